```python
import jax, jax.numpy as jnp
from jax import lax
import numpy as np

D_MODEL = 1024
BATCH = 2
SEQ = 8192
DEPTH = 2
DEC_BATCH = 32
DEC_SEQ = 4
PAST_LEN = 16384
PAGE_SIZE = 128

N_A = DEPTH // 2
N_B = DEPTH - N_A
RET_HEADS = 4
RET_DK = D_MODEL // RET_HEADS
RET_DV = 2 * RET_DK
RET_CHUNK = 128
ATT_HEADS = 8
ATT_DH = D_MODEL // ATT_HEADS
MOBA_BLOCK = 256
MOBA_TOPK = 3
Q_BLOCK = 32
D_FF = -(-8 * D_MODEL // (3 * 256)) * 256
ALPHA = (2 * DEPTH) ** 0.25
BETA = (8 * DEPTH) ** -0.25
LN_EPS = 1e-5
GN_EPS = 1e-6

kernel_name = 'yoco_retention_moba_step'


def layer_norm(x, g, b):
    xf = x.astype(jnp.float32)
    mu = xf.mean(-1, keepdims=True)
    var = jnp.square(xf - mu).mean(-1, keepdims=True)
    return ((xf - mu) * lax.rsqrt(var + LN_EPS) * g + b).astype(x.dtype)


def swiglu(x, w_in, w_out):
    g, u = jnp.split(x @ w_in, 2, axis=-1)
    return (jax.nn.silu(g) * u) @ w_out


def retention_log_decay():
    return jnp.log(1.0 - jnp.float32(2.0) ** (-5.0 - jnp.arange(RET_HEADS, dtype=jnp.float32)))


def alibi_slopes():
    return jnp.float32(2.0) ** (-8.0 * (jnp.arange(ATT_HEADS, dtype=jnp.float32) + 1.0) / ATT_HEADS)


def retention_project(x, w_in):
    B, L, _ = x.shape
    hq = RET_HEADS * RET_DK
    hv = RET_HEADS * RET_DV
    q, k, v, g = jnp.split(x @ w_in, [hq, 2 * hq, 2 * hq + hv], axis=-1)
    q = q.reshape(B, L, RET_HEADS, RET_DK).astype(jnp.float32)
    k = (k.reshape(B, L, RET_HEADS, RET_DK) * RET_DK ** -0.5).astype(jnp.float32)
    v = v.reshape(B, L, RET_HEADS, RET_DV).astype(jnp.float32)
    return q, k, v, g


def retention_chunk(S, q, k, v, lg):
    C = q.shape[1]
    i = jnp.arange(C, dtype=jnp.float32)
    diff = i[:, None] - i[None, :]
    decay = jnp.where(diff >= 0, jnp.exp(jnp.maximum(diff, 0.0)[None] * lg[:, None, None]), 0.0)
    scores = jnp.einsum('bihd,bjhd->bhij', q, k) * decay[None]
    intra = jnp.einsum('bhij,bjhe->bihe', scores, v)
    xi = jnp.exp((i[:, None] + 1.0) * lg[None, :])
    cross = jnp.einsum('bihd,bhde->bihe', q * xi[None, :, :, None], S)
    zeta = jnp.exp((C - 1.0 - i)[:, None] * lg[None, :])
    S_new = jnp.exp(C * lg)[None, :, None, None] * S + jnp.einsum('bjhd,bjhe->bhde', k * zeta[None, :, :, None], v)
    return S_new, intra + cross


def retention_prompt(q, k, v, lg):
    B, L, H, _ = q.shape
    nc = L // RET_CHUNK

    def to_chunks(a):
        return a.reshape(B, nc, RET_CHUNK, H, a.shape[-1]).transpose(1, 0, 2, 3, 4)

    S0 = jnp.zeros((B, H, RET_DK, RET_DV), jnp.float32)
    S, o = lax.scan(lambda s, c: retention_chunk(s, c[0], c[1], c[2], lg), S0,
                    (to_chunks(q), to_chunks(k), to_chunks(v)))
    return S, o.transpose(1, 0, 2, 3, 4).reshape(B, L, H, RET_DV)


def retention_output(o, g, w_out):
    B, L, H, dv = o.shape
    mu = o.mean(-1, keepdims=True)
    var = jnp.square(o - mu).mean(-1, keepdims=True)
    o = ((o - mu) * lax.rsqrt(var + GN_EPS)).reshape(B, L, H * dv).astype(g.dtype)
    return (jax.nn.silu(g) * o) @ w_out


def moba_core(q, t, ks, vs, pos, valid, slopes):
    s = jnp.einsum('bqhd,bqhnd->bqhn', q, ks).astype(jnp.float32) * ATT_DH ** -0.5
    dist = (t[None, :, None, None] - pos).astype(jnp.float32)
    s = jnp.where(valid, s - slopes[None, None, :, None] * dist, -jnp.inf)
    p = jax.nn.softmax(s, axis=-1)
    return jnp.einsum('bqhn,bqhnd->bqhd', p.astype(vs.dtype), vs)


def moba_prompt(q, k, v, slopes):
    B, L, H, dh = q.shape
    nb = -(-L // MOBA_BLOCK)
    pad = nb * MOBA_BLOCK - L
    kt = jnp.pad(k, ((0, 0), (0, pad), (0, 0), (0, 0))).reshape(B, nb, MOBA_BLOCK, H, dh).transpose(0, 3, 1, 2, 4)
    vt = jnp.pad(v, ((0, 0), (0, pad), (0, 0), (0, 0))).reshape(B, nb, MOBA_BLOCK, H, dh).transpose(0, 3, 1, 2, 4)
    means = kt.astype(jnp.float32).mean(3)
    gate = jnp.einsum('bthd,bhnd->bthn', q.astype(jnp.float32), means)
    t = jnp.arange(L)
    n_full = t // MOBA_BLOCK
    past = jnp.arange(nb)[None, :] < n_full[:, None]
    gate = jnp.where(past[None, :, None, :], gate, -jnp.inf)
    kk = min(MOBA_TOPK, nb)
    _, idx = lax.top_k(gate, kk)
    sel_valid = jnp.arange(kk)[None, :] < n_full[:, None]
    own = jnp.broadcast_to(n_full[None, :, None, None], (B, L, H, 1)).astype(idx.dtype)
    blocks = jnp.concatenate([idx, own], axis=-1)
    nblk = kk + 1
    n_keys = nblk * MOBA_BLOCK
    nq = L // Q_BLOCK
    bi = jnp.arange(B)[:, None, None, None]
    hi = jnp.arange(H)[None, None, :, None]

    def to_chunks(a):
        return a.reshape(B, nq, Q_BLOCK, H, a.shape[-1]).transpose(1, 0, 2, 3, 4)

    def attend_block(args):
        qc, bc, tc, svc = args
        ks = kt[bi, hi, bc].reshape(B, Q_BLOCK, H, n_keys, dh)
        vs = vt[bi, hi, bc].reshape(B, Q_BLOCK, H, n_keys, dh)
        pos = bc[..., None] * MOBA_BLOCK + jnp.arange(MOBA_BLOCK)
        blk_ok = jnp.concatenate([jnp.broadcast_to(svc[None, :, None, :], (B, Q_BLOCK, H, kk)),
                                  jnp.ones((B, Q_BLOCK, H, 1), bool)], axis=-1)
        valid = blk_ok[..., None] & (pos <= tc[None, :, None, None, None])
        return moba_core(qc, tc, ks, vs, pos.reshape(B, Q_BLOCK, H, n_keys),
                         valid.reshape(B, Q_BLOCK, H, n_keys), slopes)

    o = lax.map(attend_block, (to_chunks(q), to_chunks(blocks), t.reshape(nq, Q_BLOCK),
                               sel_valid.reshape(nq, Q_BLOCK, kk)))
    return o.transpose(1, 0, 2, 3, 4).reshape(B, L, H * dh)


def moba_sample(q, k_new, v_new, cache_k, cache_v, page_table, slopes):
    Bd, T, H, dh = q.shape
    n_pages = page_table.shape[1]
    past_len = n_pages * PAGE_SIZE
    ppb = MOBA_BLOCK // PAGE_SIZE
    n_full = past_len // MOBA_BLOCK
    t = past_len + jnp.arange(T)
    k_parts, v_parts, pos_parts, valid_parts = [], [], [], []
    if n_full > 0:
        kk = min(MOBA_TOPK, n_full)

        def block_means(pt_row):
            rows = cache_k[pt_row].astype(jnp.float32)
            return rows.reshape(n_full, MOBA_BLOCK, H, dh).mean(1)

        means = lax.map(block_means, page_table[:, :n_full * ppb])
        gate = jnp.einsum('bthd,bnhd->bthn', q.astype(jnp.float32), means)
        _, idx = lax.top_k(gate, kk)
        bi = jnp.arange(Bd)[:, None, None, None, None]
        hi = jnp.arange(H)[None, None, :, None, None]
        phys = page_table[bi, idx[..., None] * ppb + jnp.arange(ppb)]
        n_sel = kk * MOBA_BLOCK
        k_parts.append(cache_k[phys, :, hi, :].reshape(Bd, T, H, n_sel, dh).astype(k_new.dtype))
        v_parts.append(cache_v[phys, :, hi, :].reshape(Bd, T, H, n_sel, dh).astype(v_new.dtype))
        pos_parts.append((idx[..., None] * MOBA_BLOCK + jnp.arange(MOBA_BLOCK)).reshape(Bd, T, H, n_sel))
        valid_parts.append(jnp.ones((Bd, T, H, n_sel), bool))
    own_pages = page_table[:, n_full * ppb:]
    rem = own_pages.shape[1] * PAGE_SIZE
    own_k = jnp.concatenate([cache_k[own_pages].reshape(Bd, rem, H, dh).astype(k_new.dtype), k_new], axis=1)
    own_v = jnp.concatenate([cache_v[own_pages].reshape(Bd, rem, H, dh).astype(v_new.dtype), v_new], axis=1)
    R = rem + T
    pos_own = jnp.concatenate([n_full * MOBA_BLOCK + jnp.arange(rem), t])
    k_parts.append(jnp.broadcast_to(own_k.transpose(0, 2, 1, 3)[:, None], (Bd, T, H, R, dh)))
    v_parts.append(jnp.broadcast_to(own_v.transpose(0, 2, 1, 3)[:, None], (Bd, T, H, R, dh)))
    pos_parts.append(jnp.broadcast_to(pos_own, (Bd, T, H, R)))
    valid_parts.append(jnp.broadcast_to((pos_own[None, :] <= t[:, None])[None, :, None, :], (Bd, T, H, R)))
    o = moba_core(q, t, jnp.concatenate(k_parts, axis=3), jnp.concatenate(v_parts, axis=3),
                  jnp.concatenate(pos_parts, axis=3), jnp.concatenate(valid_parts, axis=3), slopes)
    return o.reshape(Bd, T, H * dh)


def decoder(h, retention_fn, attention_fn, w_ret_in, w_ret_out, w_kv, w_q, w_att_out,
            w_ffn_in, w_ffn_out, ln_g, ln_b):
    B, L, _ = h.shape
    states = []
    k_sh = v_sh = None
    for layer in range(DEPTH):
        if layer < N_A:
            q, k, v, g = retention_project(h, w_ret_in[layer])
            S, o = retention_fn(layer, q, k, v)
            states.append(S)
            mix = retention_output(o, g, w_ret_out[layer])
        else:
            j = layer - N_A
            if j == 0:
                k_flat, v_flat = jnp.split(h @ w_kv, 2, axis=-1)
                k_sh = k_flat.reshape(B, L, ATT_HEADS, ATT_DH)
                v_sh = v_flat.reshape(B, L, ATT_HEADS, ATT_DH)
            qb = (h @ w_q[j]).reshape(B, L, ATT_HEADS, ATT_DH)
            mix = attention_fn(qb, k_sh, v_sh) @ w_att_out[j]
        h = layer_norm(ALPHA * h + mix, ln_g[layer, 0], ln_b[layer, 0])
        h = layer_norm(ALPHA * h + swiglu(h, w_ffn_in[layer], w_ffn_out[layer]), ln_g[layer, 1], ln_b[layer, 1])
    return h, jnp.stack(states).astype(h.dtype), k_sh, v_sh


def setup_inputs(seed: int = 0) -> dict:
    key = jax.random.key(seed)
    ks = jax.random.split(key, 16)
    n_pages = PAST_LEN // PAGE_SIZE
    n_used = DEC_BATCH * n_pages
    n_pool = n_used + n_used // 4
    perm = jax.random.permutation(ks[0], n_pool)
    page_table = perm[:n_used].reshape(DEC_BATCH, n_pages).astype(jnp.int32)
    f32 = jnp.float32

    def nrm(k, shape, fan_in, scale=1.0):
        return jax.random.normal(k, shape, f32) * (scale * fan_in ** -0.5)

    ret_cols = 2 * RET_HEADS * RET_DK + 2 * RET_HEADS * RET_DV
    att_w = ATT_HEADS * ATT_DH
    return {
        'x_prompt': jax.random.normal(ks[1], (BATCH, SEQ, D_MODEL), f32),
        'x_sample': jax.random.normal(ks[2], (DEC_BATCH, DEC_SEQ, D_MODEL), f32),
        'cache_k': jax.random.normal(ks[3], (n_pool, PAGE_SIZE, ATT_HEADS, ATT_DH), f32),
        'cache_v': jax.random.normal(ks[4], (n_pool, PAGE_SIZE, ATT_HEADS, ATT_DH), f32),
        'state_ret': 0.3 * jax.random.normal(ks[5], (N_A, DEC_BATCH, RET_HEADS, RET_DK, RET_DV), f32),
        'page_table': page_table,
        'w_ret_in': nrm(ks[6], (N_A, D_MODEL, ret_cols), D_MODEL),
        'w_ret_out': nrm(ks[7], (N_A, RET_HEADS * RET_DV, D_MODEL), RET_HEADS * RET_DV, BETA),
        'w_kv': nrm(ks[8], (D_MODEL, 2 * att_w), D_MODEL),
        'w_q': nrm(ks[9], (N_B, D_MODEL, att_w), D_MODEL),
        'w_att_out': nrm(ks[10], (N_B, att_w, D_MODEL), att_w, BETA),
        'w_ffn_in': nrm(ks[11], (DEPTH, D_MODEL, 2 * D_FF), D_MODEL),
        'w_ffn_out': nrm(ks[12], (DEPTH, D_FF, D_MODEL), D_FF, BETA),
        'ln_g': 1.0 + 0.02 * jax.random.normal(ks[13], (DEPTH, 2, D_MODEL), f32),
        'ln_b': 0.02 * jax.random.normal(ks[14], (DEPTH, 2, D_MODEL), f32),
    }


def reference(x_prompt, x_sample, cache_k, cache_v, state_ret, page_table, w_ret_in, w_ret_out, w_kv,
              w_q, w_att_out, w_ffn_in, w_ffn_out, ln_g, ln_b):
    lg = retention_log_decay()
    slopes = alibi_slopes()
    y_prompt, s_prompt, k_prompt, v_prompt = decoder(
        x_prompt,
        lambda a, q, k, v: retention_prompt(q, k, v, lg),
        lambda q, k, v: moba_prompt(q, k, v, slopes),
        w_ret_in, w_ret_out, w_kv, w_q, w_att_out, w_ffn_in, w_ffn_out, ln_g, ln_b)
    y_sample, s_sample, k_sample, v_sample = decoder(
        x_sample,
        lambda a, q, k, v: retention_chunk(state_ret[a].astype(jnp.float32), q, k, v, lg),
        lambda q, k, v: moba_sample(q, k, v, cache_k, cache_v, page_table, slopes),
        w_ret_in, w_ret_out, w_kv, w_q, w_att_out, w_ffn_in, w_ffn_out, ln_g, ln_b)
    return (y_prompt, y_sample, s_prompt, s_sample, k_prompt, v_prompt, k_sample, v_sample)
```

```python
import functools
import math

import jax
import jax.numpy as jnp
from jax import lax
from jax.experimental import pallas as pl
from jax.experimental.pallas import tpu as pltpu

F32 = jnp.float32
BF16 = jnp.bfloat16

RET_HEADS = 4
RET_CHUNK = 128
ATT_HEADS = 8
MOBA_BLOCK = 256
MOBA_TOPK = 3
PAGE_SIZE = 128
LN_EPS = 1e-5
GN_EPS = 1e-6
MASKED = -1e30
V7X_VMEM_LIMIT_BYTES = 56 * 1024 * 1024
SAMPLE_CHUNK_PAD = 16
LANES = 128

NT_DIMS = (((1,), (1,)), ((), ()))
TN_DIMS = (((0,), (0,)), ((), ()))


def _cparams(*sem):
    return pltpu.CompilerParams(dimension_semantics=sem, vmem_limit_bytes=V7X_VMEM_LIMIT_BYTES)


def _row_tile(m):
    for t in (512, 256, 128, 64, 32, 16, 8):
        if m % t == 0:
            return t
    return m


def _proj_kernel(x_ref, w_ref, *o_refs, segs, chunk):
    xb = x_ref[...].astype(BF16)
    oi = 0
    for start, width, scale, dtypes in segs:
        for c0 in range(0, width, chunk):
            cw = min(chunk, width - c0)
            acc = jnp.dot(xb, w_ref[:, start + c0:start + c0 + cw], preferred_element_type=F32)
            if scale != 1.0:
                acc = acc * scale
            for d, dt in enumerate(dtypes):
                o_refs[oi + d][:, c0:c0 + cw] = acc.astype(dt)
        oi += len(dtypes)


def _proj(x, w, segs):
    m, kdim = x.shape
    tm = _row_tile(m)
    out_shape, out_specs = [], []
    for _, width, _, dtypes in segs:
        for dt in dtypes:
            out_shape.append(jax.ShapeDtypeStruct((m, width), dt))
            out_specs.append(pl.BlockSpec((tm, width), lambda i: (i, 0)))
    return pl.pallas_call(
        functools.partial(_proj_kernel, segs=segs, chunk=512),
        grid=(m // tm,),
        in_specs=[pl.BlockSpec((tm, kdim), lambda i: (i, 0)),
                  pl.BlockSpec(w.shape, lambda i: (0, 0))],
        out_specs=out_specs, out_shape=out_shape,
        compiler_params=_cparams("parallel"), name="proj",
    )(x, w)


def _outproj_ln_kernel(a_ref, w_ref, r_ref, g_ref, b_ref, o_ref, *, alpha):
    acc = jnp.dot(a_ref[...], w_ref[...], preferred_element_type=F32)
    y = alpha * r_ref[...] + acc
    mu = jnp.mean(y, axis=-1, keepdims=True)
    d = y - mu
    var = jnp.mean(d * d, axis=-1, keepdims=True)
    o_ref[...] = d * lax.rsqrt(var + LN_EPS) * g_ref[...] + b_ref[...]


def _outproj_ln(a, w, res, g, b, alpha):
    m, kdim = a.shape
    dm = w.shape[1]
    tm = _row_tile(m)
    return pl.pallas_call(
        functools.partial(_outproj_ln_kernel, alpha=alpha),
        grid=(m // tm,),
        in_specs=[pl.BlockSpec((tm, kdim), lambda i: (i, 0)),
                  pl.BlockSpec((kdim, dm), lambda i: (0, 0)),
                  pl.BlockSpec((tm, dm), lambda i: (i, 0)),
                  pl.BlockSpec((1, dm), lambda i: (0, 0)),
                  pl.BlockSpec((1, dm), lambda i: (0, 0))],
        out_specs=pl.BlockSpec((tm, dm), lambda i: (i, 0)),
        out_shape=jax.ShapeDtypeStruct((m, dm), F32),
        compiler_params=_cparams("parallel"), name="outproj_ln",
    )(a, w, res, g.reshape(1, dm), b.reshape(1, dm))


def _ffn_in_kernel(h_ref, w_ref, o_ref, *, dff, chunk):
    hb = h_ref[...].astype(BF16)
    for c0 in range(0, dff, chunk):
        cw = min(chunk, dff - c0)
        gate = jnp.dot(hb, w_ref[:, c0:c0 + cw], preferred_element_type=F32)
        up = jnp.dot(hb, w_ref[:, dff + c0:dff + c0 + cw], preferred_element_type=F32)
        o_ref[:, c0:c0 + cw] = (gate * jax.nn.sigmoid(gate) * up).astype(o_ref.dtype)


def _ffn_in(h, w):
    m, dm = h.shape
    dff = w.shape[1] // 2
    tm = _row_tile(m)
    return pl.pallas_call(
        functools.partial(_ffn_in_kernel, dff=dff, chunk=512),
        grid=(m // tm,),
        in_specs=[pl.BlockSpec((tm, dm), lambda i: (i, 0)),
                  pl.BlockSpec(w.shape, lambda i: (0, 0))],
        out_specs=pl.BlockSpec((tm, dff), lambda i: (i, 0)),
        out_shape=jax.ShapeDtypeStruct((m, dff), BF16),
        compiler_params=_cparams("parallel"), name="ffn_in",
    )(h, w)


def _retention_kernel(*refs, heads, c_real, has_init):
    if has_init:
        q_ref, k_ref, v_ref, g_ref, s0_ref, o_ref, s_ref = refs
    else:
        q_ref, k_ref, v_ref, g_ref, o_ref, s_ref = refs

    @pl.when(pl.program_id(1) == 0)
    def _():
        if has_init:
            s_ref[...] = s0_ref[...]
        else:
            s_ref[...] = jnp.zeros(s_ref.shape, F32)

    cp = q_ref.shape[1]
    dk = q_ref.shape[2] // heads
    dv = v_ref.shape[2] // heads
    row = lax.broadcasted_iota(jnp.int32, (cp, cp), 0)
    col = lax.broadcasted_iota(jnp.int32, (cp, cp), 1)
    diff = (row - col).astype(F32)
    ri = lax.broadcasted_iota(jnp.int32, (cp, 1), 0).astype(F32)
    for h in range(heads):
        lg = math.log(1.0 - 2.0 ** (-5.0 - h))
        decay = jnp.where(diff >= 0, jnp.exp(jnp.maximum(diff, 0.0) * lg), 0.0)
        xi = jnp.exp((ri + 1.0) * lg)
        zeta = jnp.exp((c_real - 1.0 - ri) * lg)
        qh = q_ref[0, :, h * dk:(h + 1) * dk]
        kh = k_ref[0, :, h * dk:(h + 1) * dk]
        vh = v_ref[0, :, h * dv:(h + 1) * dv]
        s_old = s_ref[0, h]
        scores = lax.dot_general(qh, kh, NT_DIMS, preferred_element_type=F32) * decay
        intra = jnp.dot(scores.astype(BF16), vh, preferred_element_type=F32)
        cross = jnp.dot(qh, s_old.astype(BF16), preferred_element_type=F32) * xi
        kz = (kh.astype(F32) * zeta).astype(BF16)
        upd = lax.dot_general(kz, vh, TN_DIMS, preferred_element_type=F32)
        s_ref[0, h] = math.exp(c_real * lg) * s_old + upd
        o = intra + cross
        mu = jnp.mean(o, axis=-1, keepdims=True)
        d = o - mu
        var = jnp.mean(d * d, axis=-1, keepdims=True)
        on = d * lax.rsqrt(var + GN_EPS)
        gh = g_ref[0, :, h * dv:(h + 1) * dv].astype(F32)
        o_ref[0, :, h * dv:(h + 1) * dv] = (gh * jax.nn.sigmoid(gh) * on).astype(o_ref.dtype)


def _retention(q, k, v, g, s0, *, n_seq, c_real):
    n_chunks, cp, qw = q.shape
    vw = v.shape[2]
    nc = n_chunks // n_seq
    dk, dv = qw // RET_HEADS, vw // RET_HEADS
    tok_map = lambda b, c: (b * nc + c, 0, 0)
    state_spec = pl.BlockSpec((1, RET_HEADS, dk, dv), lambda b, c: (b, 0, 0, 0))
    in_specs = [pl.BlockSpec((1, cp, qw), tok_map), pl.BlockSpec((1, cp, qw), tok_map),
                pl.BlockSpec((1, cp, vw), tok_map), pl.BlockSpec((1, cp, vw), tok_map)]
    args = [q, k, v, g]
    if s0 is not None:
        in_specs.append(state_spec)
        args.append(s0)
    return pl.pallas_call(
        functools.partial(_retention_kernel, heads=RET_HEADS, c_real=c_real, has_init=s0 is not None),
        grid=(n_seq, nc),
        in_specs=in_specs,
        out_specs=[pl.BlockSpec((1, cp, vw), tok_map), state_spec],
        out_shape=[jax.ShapeDtypeStruct((n_chunks, cp, vw), BF16),
                   jax.ShapeDtypeStruct((n_seq, RET_HEADS, dk, dv), F32)],
        compiler_params=_cparams("parallel", "arbitrary"), name="retention",
    )(*args)


def _block_means_kernel(k_ref, o_ref, *, blk):
    for i in range(o_ref.shape[0]):
        o_ref[i:i + 1, :] = jnp.mean(k_ref[i * blk:(i + 1) * blk, :], axis=0, keepdims=True)


def _block_means(k2d, blk):
    m, w = k2d.shape
    n_blocks = m // blk
    per_step = 8 if n_blocks % 8 == 0 else n_blocks
    return pl.pallas_call(
        functools.partial(_block_means_kernel, blk=blk),
        grid=(n_blocks // per_step,),
        in_specs=[pl.BlockSpec((per_step * blk, w), lambda i: (i, 0))],
        out_specs=pl.BlockSpec((per_step, w), lambda i: (i, 0)),
        out_shape=jax.ShapeDtypeStruct((n_blocks, w), F32),
        compiler_params=_cparams("parallel"), name="block_means",
    )(k2d)


def _moba_prompt_kernel(q_ref, k_ref, v_ref, mean_ref, slope_ref, o_ref,
                        vt_ref, sel_ref, m_ref, l_ref, acc_ref, *, blk, topk, scale):
    i = pl.program_id(2)
    nb = k_ref.shape[0] // blk

    @pl.when(i == 0)
    def _():
        for jb in range(nb):
            vt_ref[jb] = v_ref[jb * blk:(jb + 1) * blk, :].astype(F32).T.astype(BF16)

    slope = slope_ref[0][:, :1]
    qt = q_ref[...].astype(F32).T.astype(BF16)

    gate = jnp.dot(mean_ref[...].astype(BF16), qt, preferred_element_type=F32)
    bidx = lax.broadcasted_iota(jnp.int32, (nb, blk), 0)
    past = bidx < i
    gm = jnp.where(past, gate, -jnp.inf)
    cnt = jnp.zeros((nb, blk), jnp.int32)
    for jp in range(nb):
        row = gm[jp:jp + 1, :]
        beats = (row > gm) | ((row == gm) & (jp < bidx))
        cnt = cnt + beats.astype(jnp.int32)
    sel_ref[...] = jnp.where(past & (cnt < topk), 0.0, MASKED)

    key_i = lax.broadcasted_iota(jnp.int32, (blk, blk), 0)
    qry_i = lax.broadcasted_iota(jnp.int32, (blk, blk), 1)
    dq = (qry_i - key_i).astype(F32)
    dbias = dq * slope

    k_own = k_ref[pl.ds(pl.multiple_of(i * blk, blk), blk), :]
    s = jnp.dot(k_own, qt, preferred_element_type=F32) * scale - dbias
    s = jnp.where(dq >= 0, s, MASKED)
    m0 = jnp.max(s, axis=0, keepdims=True)
    p = jnp.exp(s - m0)
    m_ref[...] = m0
    l_ref[...] = jnp.sum(p, axis=0, keepdims=True)
    acc_ref[...] = jnp.dot(vt_ref[i], p.astype(BF16), preferred_element_type=F32)

    def body(j, carry):
        kj = k_ref[pl.ds(pl.multiple_of(j * blk, blk), blk), :]
        off = jnp.full((1, 1), (i - j) * blk, jnp.int32).astype(F32)
        sj = jnp.dot(kj, qt, preferred_element_type=F32) * scale - (dbias + off * slope)
        sj = sj + sel_ref[pl.ds(j, 1), :]
        m_old = m_ref[...]
        m_new = jnp.maximum(m_old, jnp.max(sj, axis=0, keepdims=True))
        alpha = jnp.exp(m_old - m_new)
        pj = jnp.exp(sj - m_new)
        m_ref[...] = m_new
        l_ref[...] = alpha * l_ref[...] + jnp.sum(pj, axis=0, keepdims=True)
        acc_ref[...] = alpha * acc_ref[...] + jnp.dot(vt_ref[j], pj.astype(BF16), preferred_element_type=F32)
        return carry

    lax.fori_loop(0, i, body, 0)
    o_ref[...] = (acc_ref[...] / l_ref[...]).T.astype(o_ref.dtype)


def _moba_prompt(q, k, v, means, slopes, *, n_seq):
    m, w = q.shape
    seq = m // n_seq
    dh = w // ATT_HEADS
    blk = MOBA_BLOCK
    nb = seq // blk
    return pl.pallas_call(
        functools.partial(_moba_prompt_kernel, blk=blk, topk=MOBA_TOPK, scale=dh ** -0.5),
        grid=(n_seq, ATT_HEADS, nb),
        in_specs=[pl.BlockSpec((blk, dh), lambda b, h, i: (b * nb + i, h)),
                  pl.BlockSpec((seq, dh), lambda b, h, i: (b, h)),
                  pl.BlockSpec((seq, dh), lambda b, h, i: (b, h)),
                  pl.BlockSpec((nb, dh), lambda b, h, i: (b, h)),
                  pl.BlockSpec((1, 1, LANES), lambda b, h, i: (h, 0, 0))],
        out_specs=pl.BlockSpec((blk, dh), lambda b, h, i: (b * nb + i, h)),
        out_shape=jax.ShapeDtypeStruct((m, w), BF16),
        scratch_shapes=[pltpu.VMEM((nb, dh, blk), BF16),
                        pltpu.VMEM((nb, blk), F32),
                        pltpu.VMEM((1, blk), F32),
                        pltpu.VMEM((1, blk), F32),
                        pltpu.VMEM((dh, blk), F32)],
        compiler_params=_cparams("arbitrary", "arbitrary", "arbitrary"), name="moba_prompt",
    )(q, k, v, means, slopes)


def _cache_means_kernel(pt_ref, *refs, ppb):
    del pt_ref
    o_ref = refs[-1]
    n_grp = (len(refs) - 1) // ppb
    for gi in range(n_grp):
        acc = jnp.sum(refs[gi * ppb][0], axis=0)
        for p in range(1, ppb):
            acc = acc + jnp.sum(refs[gi * ppb + p][0], axis=0)
        o_ref[0, gi] = acc * (1.0 / (ppb * refs[0].shape[1]))


def _cache_means(cache_k, page_table, n_full, ppb):
    n_seq = page_table.shape[0]
    _, ps, heads, dh = cache_k.shape
    grp = 4 if n_full % 4 == 0 else 1

    def page_map(b, n, pt, *, gi, p):
        return (pt[b, (n * grp + gi) * ppb + p], 0, 0, 0)

    in_specs = [pl.BlockSpec((1, ps, heads, dh), functools.partial(page_map, gi=gi, p=p))
                for gi in range(grp) for p in range(ppb)]
    return pl.pallas_call(
        functools.partial(_cache_means_kernel, ppb=ppb),
        grid_spec=pltpu.PrefetchScalarGridSpec(
            num_scalar_prefetch=1, grid=(n_seq, n_full // grp), in_specs=in_specs,
            out_specs=pl.BlockSpec((1, grp, heads, dh), lambda b, n, pt: (b, n, 0, 0))),
        out_shape=jax.ShapeDtypeStruct((n_seq, n_full, heads, dh), F32),
        compiler_params=_cparams("parallel", "arbitrary"), name="cache_means",
    )(page_table, *([cache_k] * (grp * ppb)))


def _sample_topk_kernel(q_ref, mean_ref, o_ref, *, heads, topk):
    t_n = q_ref.shape[1]
    dh = q_ref.shape[2] // heads
    nb = mean_ref.shape[1]
    lane = lax.broadcasted_iota(jnp.int32, (t_n, LANES), 1)
    bidx = lax.broadcasted_iota(jnp.int32, (t_n, nb), 1)
    out = jnp.zeros((t_n, LANES), jnp.int32)
    for h in range(heads):
        qh = q_ref[0, :, h * dh:(h + 1) * dh]
        mh = mean_ref[0, :, h * dh:(h + 1) * dh]
        gate = lax.dot_general(qh, mh, NT_DIMS, preferred_element_type=F32)
        cnt = jnp.zeros((t_n, nb), jnp.int32)
        for jp in range(nb):
            colv = gate[:, jp:jp + 1]
            beats = (colv > gate) | ((colv == gate) & (jp < bidx))
            cnt = cnt + beats.astype(jnp.int32)
        for r in range(topk):
            idx_r = jnp.sum(jnp.where(cnt == r, bidx, 0), axis=1, keepdims=True)
            out = jnp.where(lane == h * topk + r, idx_r, out)
    o_ref[0] = out


def _sample_topk(q3, means3):
    n_seq, t_n, w = q3.shape
    nb = means3.shape[1]
    return pl.pallas_call(
        functools.partial(_sample_topk_kernel, heads=ATT_HEADS, topk=MOBA_TOPK),
        grid=(n_seq,),
        in_specs=[pl.BlockSpec((1, t_n, w), lambda b: (b, 0, 0)),
                  pl.BlockSpec((1, nb, w), lambda b: (b, 0, 0))],
        out_specs=pl.BlockSpec((1, t_n, LANES), lambda b: (b, 0, 0)),
        out_shape=jax.ShapeDtypeStruct((n_seq, t_n, LANES), jnp.int32),
        compiler_params=_cparams("parallel"), name="sample_topk",
    )(q3, means3)


def _moba_sample_kernel(idx_ref, pt_ref, q_ref, kn_ref, vn_ref, slope_ref, *refs,
                        heads, topk, ppb, past_len, blk, scale):
    del pt_ref
    t_n = q_ref.shape[1]
    n_slots = t_n * topk * ppb
    k_refs, v_refs, o_ref = refs[:n_slots], refs[n_slots:2 * n_slots], refs[2 * n_slots]
    ps = k_refs[0].shape[1]
    b = pl.program_id(0)
    h = pl.program_id(1)
    q = q_ref[0]
    slope = slope_ref[0][:, :1]
    row_i = lax.broadcasted_iota(jnp.int32, (t_n, ps), 0)
    lane_i = lax.broadcasted_iota(jnp.int32, (t_n, ps), 1)

    scores = []
    for t in range(t_n):
        for r in range(topk):
            blk_idx = idx_ref[((b * t_n + t) * heads + h) * topk + r]
            for p in range(ppb):
                kp = k_refs[(t * topk + r) * ppb + p][0]
                s = lax.dot_general(q, kp, NT_DIMS, preferred_element_type=F32) * scale
                dist0 = past_len + t - blk_idx * blk - p * ps
                s = s - slope * (dist0 - lane_i).astype(F32)
                scores.append(jnp.where(row_i == t, s, MASKED))

    ro = lax.broadcasted_iota(jnp.int32, (t_n, t_n), 0)
    co = lax.broadcasted_iota(jnp.int32, (t_n, t_n), 1)
    s_own = lax.dot_general(q, kn_ref[0], NT_DIMS, preferred_element_type=F32) * scale
    s_own = s_own - slope * (ro - co).astype(F32)
    s_own = jnp.where(co <= ro, s_own, MASKED)

    m_el = scores[0]
    for s in scores[1:]:
        m_el = jnp.maximum(m_el, s)
    m = jnp.maximum(jnp.max(m_el, axis=1, keepdims=True), jnp.max(s_own, axis=1, keepdims=True))
    p_own = jnp.exp(s_own - m)
    acc = jnp.dot(p_own, vn_ref[0], preferred_element_type=F32)
    l_el = jnp.zeros((t_n, ps), F32)
    for slot, s in enumerate(scores):
        pe = jnp.exp(s - m)
        l_el = l_el + pe
        acc = acc + jnp.dot(pe, v_refs[slot][0], preferred_element_type=F32)
    l = jnp.sum(l_el, axis=1, keepdims=True) + jnp.sum(p_own, axis=1, keepdims=True)
    o_ref[0] = acc / l


def _moba_sample(q3, kn3, vn3, cache_k, cache_v, page_table, idx_flat, slopes, *, past_len):
    n_seq, t_n, w = q3.shape
    n_pool, ps, heads, dh = cache_k.shape
    ppb = MOBA_BLOCK // ps
    ck = cache_k.reshape(n_pool, ps, heads * dh)
    cv = cache_v.reshape(n_pool, ps, heads * dh)

    def page_map(b, h, idx, pt, *, t, r, p):
        blk_idx = idx[((b * t_n + t) * heads + h) * MOBA_TOPK + r]
        return (pt[b, blk_idx * ppb + p], 0, h)

    slot_specs = [pl.BlockSpec((1, ps, dh), functools.partial(page_map, t=t, r=r, p=p))
                  for t in range(t_n) for r in range(MOBA_TOPK) for p in range(ppb)]
    tok_spec = pl.BlockSpec((1, t_n, dh), lambda b, h, idx, pt: (b, 0, h))
    n_slots = len(slot_specs)
    return pl.pallas_call(
        functools.partial(_moba_sample_kernel, heads=heads, topk=MOBA_TOPK, ppb=ppb, past_len=past_len,
                          blk=MOBA_BLOCK, scale=dh ** -0.5),
        grid_spec=pltpu.PrefetchScalarGridSpec(
            num_scalar_prefetch=2, grid=(n_seq, heads),
            in_specs=[tok_spec, tok_spec, tok_spec,
                      pl.BlockSpec((1, 1, LANES), lambda b, h, idx, pt: (h, 0, 0))] + slot_specs + slot_specs,
            out_specs=tok_spec),
        out_shape=jax.ShapeDtypeStruct((n_seq, t_n, w), F32),
        compiler_params=_cparams("parallel", "arbitrary"), name="moba_sample",
    )(idx_flat, page_table, q3, kn3, vn3, slopes, *([ck] * n_slots), *([cv] * n_slots))


def _alibi_slope_rows():
    s = jnp.float32(2.0) ** (-8.0 * (jnp.arange(ATT_HEADS, dtype=F32) + 1.0) / ATT_HEADS)
    return jnp.broadcast_to(s[:, None, None], (ATT_HEADS, 1, LANES))


def _decoder(x, retention_fn, attention_fn, w, depth, is_prompt):
    n_seq, seq, dm = x.shape
    m = n_seq * seq
    n_ret = depth // 2
    alpha = (2 * depth) ** 0.25
    hq = w["ret_in"][0].shape[1] // 6
    dk = hq // RET_HEADS
    aw = w["kv"].shape[1] // 2
    h = x.reshape(m, dm)
    states = []
    k_f32 = v_f32 = k_b = v_b = None
    for layer in range(depth):
        if layer < n_ret:
            segs = ((0, hq, 1.0, (BF16,)), (hq, hq, dk ** -0.5, (BF16,)),
                    (2 * hq, 2 * hq, 1.0, (BF16,)), (4 * hq, 2 * hq, 1.0, (BF16,)))
            q, k, v, g = _proj(h, w["ret_in"][layer], segs)
            mix_in, s_new = retention_fn(layer, q, k, v, g)
            states.append(s_new)
            w_mix = w["ret_out"][layer]
        else:
            j = layer - n_ret
            if j == 0:
                k_dt = (F32, BF16) if is_prompt else (F32,)
                q_dt = (BF16,) if is_prompt else (F32,)
                segs = ((0, aw, 1.0, k_dt), (aw, aw, 1.0, k_dt), (2 * aw, aw, 1.0, q_dt))
                outs = _proj(h, jnp.concatenate([w["kv"], w["q"][j]], axis=1), segs)
                if is_prompt:
                    k_f32, k_b, v_f32, v_b, qa = outs
                else:
                    k_f32, v_f32, qa = outs
            else:
                (qa,) = _proj(h, w["q"][j], ((0, aw, 1.0, (BF16,) if is_prompt else (F32,)),))
            mix_in = attention_fn(qa, k_f32, v_f32, k_b, v_b)
            w_mix = w["att_out"][j]
        h = _outproj_ln(mix_in, w_mix, h, w["ln_g"][layer, 0], w["ln_b"][layer, 0], alpha)
        f = _ffn_in(h, w["ffn_in"][layer])
        h = _outproj_ln(f, w["ffn_out"][layer], h, w["ln_g"][layer, 1], w["ln_b"][layer, 1], alpha)
    return h.reshape(n_seq, seq, dm), jnp.stack(states), k_f32, v_f32


def kernel(x_prompt, x_sample, cache_k, cache_v, state_ret, page_table, w_ret_in, w_ret_out, w_kv,
           w_q, w_att_out, w_ffn_in, w_ffn_out, ln_g, ln_b):
    depth = w_ffn_in.shape[0]
    n_b, seq, _ = x_prompt.shape
    n_d, t_n, _ = x_sample.shape
    ps = cache_k.shape[1]
    n_pages = page_table.shape[1]
    past_len = n_pages * ps
    ppb = MOBA_BLOCK // ps
    n_full = past_len // MOBA_BLOCK
    assert seq % MOBA_BLOCK == 0 and seq % RET_CHUNK == 0
    assert n_full * ppb == n_pages and n_full >= MOBA_TOPK
    assert t_n <= SAMPLE_CHUNK_PAD

    w = {"ret_in": w_ret_in.astype(BF16), "ret_out": w_ret_out.astype(BF16), "kv": w_kv.astype(BF16),
         "q": w_q.astype(BF16), "att_out": w_att_out.astype(BF16), "ffn_in": w_ffn_in.astype(BF16),
         "ffn_out": w_ffn_out.astype(BF16), "ln_g": ln_g, "ln_b": ln_b}
    slopes = _alibi_slope_rows()

    def ret_prompt(layer, q, k, v, g):
        del layer
        c3 = lambda a: a.reshape(-1, RET_CHUNK, a.shape[-1])
        o, s = _retention(c3(q), c3(k), c3(v), c3(g), None, n_seq=n_b, c_real=RET_CHUNK)
        return o.reshape(n_b * seq, -1), s

    def att_prompt(q, k_f32, v_f32, k_b, v_b):
        del v_f32
        means = _block_means(k_f32, MOBA_BLOCK)
        return _moba_prompt(q, k_b, v_b, means, slopes, n_seq=n_b)

    y_p, s_p, k_p, v_p = _decoder(x_prompt, ret_prompt, att_prompt, w, depth, True)

    def ret_sample(layer, q, k, v, g):
        def c3(a):
            a = a.reshape(n_d, t_n, a.shape[-1])
            return jnp.pad(a, ((0, 0), (0, SAMPLE_CHUNK_PAD - t_n), (0, 0)))
        o, s = _retention(c3(q), c3(k), c3(v), c3(g), state_ret[layer].astype(F32), n_seq=n_d, c_real=t_n)
        return o[:, :t_n].reshape(n_d * t_n, -1), s

    def att_sample(q, k_f32, v_f32, k_b, v_b):
        del k_b, v_b
        r3 = lambda a: a.reshape(n_d, t_n, a.shape[-1])
        means = _cache_means(cache_k, page_table, n_full, ppb)
        idx = _sample_topk(r3(q), means.reshape(n_d, n_full, -1))
        idx_flat = idx[:, :, :ATT_HEADS * MOBA_TOPK].reshape(-1)
        o = _moba_sample(r3(q), r3(k_f32), r3(v_f32), cache_k, cache_v, page_table, idx_flat, slopes,
                         past_len=past_len)
        return o.reshape(n_d * t_n, -1).astype(BF16)

    y_s, s_s, k_s, v_s = _decoder(x_sample, ret_sample, att_sample, w, depth, False)

    dh = k_p.shape[-1] // ATT_HEADS
    kv4 = lambda a, n, l: a.reshape(n, l, ATT_HEADS, dh)
    return (y_p, y_s, s_p, s_s, kv4(k_p, n_b, seq), kv4(v_p, n_b, seq), kv4(k_s, n_d, t_n), kv4(v_s, n_d, t_n))
```

```python
import functools
import math

import jax
import jax.numpy as jnp
from jax import lax
from jax.experimental import pallas as pl
from jax.experimental.pallas import tpu as pltpu

F32 = jnp.float32
BF16 = jnp.bfloat16

RET_HEADS = 4
RET_CHUNK = 128
ATT_HEADS = 8
MOBA_BLOCK = 256
MOBA_TOPK = 3
PAGE_SIZE = 128
LN_EPS = 1e-5
GN_EPS = 1e-6
MASKED = -1e30
V7X_VMEM_LIMIT_BYTES = 56 * 1024 * 1024
SAMPLE_CHUNK_PAD = 16
LANES = 128
MOBA_KV_GROUP = 4

NT_DIMS = (((1,), (1,)), ((), ()))
TN_DIMS = (((0,), (0,)), ((), ()))


def _cparams(*sem):
    return pltpu.CompilerParams(dimension_semantics=sem, vmem_limit_bytes=V7X_VMEM_LIMIT_BYTES)


def _row_tile(m):
    for t in (512, 256, 128, 64, 32, 16, 8):
        if m % t == 0:
            return t
    return m


def _proj_kernel(x_ref, w_ref, *o_refs, segs, chunk):
    xb = x_ref[...].astype(BF16)
    oi = 0
    for start, width, scale, dtypes in segs:
        for c0 in range(0, width, chunk):
            cw = min(chunk, width - c0)
            acc = jnp.dot(xb, w_ref[:, start + c0:start + c0 + cw], preferred_element_type=F32)
            if scale != 1.0:
                acc = acc * scale
            for d, dt in enumerate(dtypes):
                o_refs[oi + d][:, c0:c0 + cw] = acc.astype(dt)
        oi += len(dtypes)


def _proj(x, w, segs):
    m, kdim = x.shape
    tm = _row_tile(m)
    out_shape, out_specs = [], []
    for _, width, _, dtypes in segs:
        for dt in dtypes:
            out_shape.append(jax.ShapeDtypeStruct((m, width), dt))
            out_specs.append(pl.BlockSpec((tm, width), lambda i: (i, 0)))
    return pl.pallas_call(
        functools.partial(_proj_kernel, segs=segs, chunk=512),
        grid=(m // tm,),
        in_specs=[pl.BlockSpec((tm, kdim), lambda i: (i, 0)),
                  pl.BlockSpec(w.shape, lambda i: (0, 0))],
        out_specs=out_specs, out_shape=out_shape,
        compiler_params=_cparams("parallel"), name="proj",
    )(x, w)


def _outproj_ln_kernel(a_ref, w_ref, r_ref, g_ref, b_ref, o_ref, *, alpha):
    acc = jnp.dot(a_ref[...], w_ref[...], preferred_element_type=F32)
    y = alpha * r_ref[...] + acc
    mu = jnp.mean(y, axis=-1, keepdims=True)
    d = y - mu
    var = jnp.mean(d * d, axis=-1, keepdims=True)
    o_ref[...] = d * lax.rsqrt(var + LN_EPS) * g_ref[...] + b_ref[...]


def _outproj_ln(a, w, res, g, b, alpha):
    m, kdim = a.shape
    dm = w.shape[1]
    tm = _row_tile(m)
    return pl.pallas_call(
        functools.partial(_outproj_ln_kernel, alpha=alpha),
        grid=(m // tm,),
        in_specs=[pl.BlockSpec((tm, kdim), lambda i: (i, 0)),
                  pl.BlockSpec((kdim, dm), lambda i: (0, 0)),
                  pl.BlockSpec((tm, dm), lambda i: (i, 0)),
                  pl.BlockSpec((1, dm), lambda i: (0, 0)),
                  pl.BlockSpec((1, dm), lambda i: (0, 0))],
        out_specs=pl.BlockSpec((tm, dm), lambda i: (i, 0)),
        out_shape=jax.ShapeDtypeStruct((m, dm), F32),
        compiler_params=_cparams("parallel"), name="outproj_ln",
    )(a, w, res, g.reshape(1, dm), b.reshape(1, dm))


def _ffn_in_kernel(h_ref, w_ref, o_ref, *, dff, chunk):
    hb = h_ref[...].astype(BF16)
    for c0 in range(0, dff, chunk):
        cw = min(chunk, dff - c0)
        gate = jnp.dot(hb, w_ref[:, c0:c0 + cw], preferred_element_type=F32)
        up = jnp.dot(hb, w_ref[:, dff + c0:dff + c0 + cw], preferred_element_type=F32)
        o_ref[:, c0:c0 + cw] = (gate * jax.nn.sigmoid(gate) * up).astype(o_ref.dtype)


def _ffn_in(h, w):
    m, dm = h.shape
    dff = w.shape[1] // 2
    tm = _row_tile(m)
    return pl.pallas_call(
        functools.partial(_ffn_in_kernel, dff=dff, chunk=512),
        grid=(m // tm,),
        in_specs=[pl.BlockSpec((tm, dm), lambda i: (i, 0)),
                  pl.BlockSpec(w.shape, lambda i: (0, 0))],
        out_specs=pl.BlockSpec((tm, dff), lambda i: (i, 0)),
        out_shape=jax.ShapeDtypeStruct((m, dff), BF16),
        compiler_params=_cparams("parallel"), name="ffn_in",
    )(h, w)


def _retention_kernel(*refs, heads, c_real, has_init):
    if has_init:
        q_ref, k_ref, v_ref, g_ref, s0_ref, o_ref, s_ref = refs
    else:
        q_ref, k_ref, v_ref, g_ref, o_ref, s_ref = refs

    @pl.when(pl.program_id(1) == 0)
    def _():
        if has_init:
            s_ref[...] = s0_ref[...]
        else:
            s_ref[...] = jnp.zeros(s_ref.shape, F32)

    cp = q_ref.shape[1]
    dk = q_ref.shape[2] // heads
    dv = v_ref.shape[2] // heads
    row = lax.broadcasted_iota(jnp.int32, (cp, cp), 0)
    col = lax.broadcasted_iota(jnp.int32, (cp, cp), 1)
    diff = (row - col).astype(F32)
    ri = lax.broadcasted_iota(jnp.int32, (cp, 1), 0).astype(F32)
    for h in range(heads):
        lg = math.log(1.0 - 2.0 ** (-5.0 - h))
        decay = jnp.where(diff >= 0, jnp.exp(jnp.maximum(diff, 0.0) * lg), 0.0)
        xi = jnp.exp((ri + 1.0) * lg)
        zeta = jnp.exp((c_real - 1.0 - ri) * lg)
        qh = q_ref[0, :, h * dk:(h + 1) * dk]
        kh = k_ref[0, :, h * dk:(h + 1) * dk]
        vh = v_ref[0, :, h * dv:(h + 1) * dv]
        s_old = s_ref[0, h]
        scores = lax.dot_general(qh, kh, NT_DIMS, preferred_element_type=F32) * decay
        intra = jnp.dot(scores.astype(BF16), vh, preferred_element_type=F32)
        cross = jnp.dot(qh, s_old.astype(BF16), preferred_element_type=F32) * xi
        kz = (kh.astype(F32) * zeta).astype(BF16)
        upd = lax.dot_general(kz, vh, TN_DIMS, preferred_element_type=F32)
        s_ref[0, h] = math.exp(c_real * lg) * s_old + upd
        o = intra + cross
        mu = jnp.mean(o, axis=-1, keepdims=True)
        d = o - mu
        var = jnp.mean(d * d, axis=-1, keepdims=True)
        on = d * lax.rsqrt(var + GN_EPS)
        gh = g_ref[0, :, h * dv:(h + 1) * dv].astype(F32)
        o_ref[0, :, h * dv:(h + 1) * dv] = (gh * jax.nn.sigmoid(gh) * on).astype(o_ref.dtype)


def _retention(q, k, v, g, s0, *, n_seq, c_real):
    n_chunks, cp, qw = q.shape
    vw = v.shape[2]
    nc = n_chunks // n_seq
    dk, dv = qw // RET_HEADS, vw // RET_HEADS
    tok_map = lambda b, c: (b * nc + c, 0, 0)
    state_spec = pl.BlockSpec((1, RET_HEADS, dk, dv), lambda b, c: (b, 0, 0, 0))
    in_specs = [pl.BlockSpec((1, cp, qw), tok_map), pl.BlockSpec((1, cp, qw), tok_map),
                pl.BlockSpec((1, cp, vw), tok_map), pl.BlockSpec((1, cp, vw), tok_map)]
    args = [q, k, v, g]
    if s0 is not None:
        in_specs.append(state_spec)
        args.append(s0)
    return pl.pallas_call(
        functools.partial(_retention_kernel, heads=RET_HEADS, c_real=c_real, has_init=s0 is not None),
        grid=(n_seq, nc),
        in_specs=in_specs,
        out_specs=[pl.BlockSpec((1, cp, vw), tok_map), state_spec],
        out_shape=[jax.ShapeDtypeStruct((n_chunks, cp, vw), BF16),
                   jax.ShapeDtypeStruct((n_seq, RET_HEADS, dk, dv), F32)],
        compiler_params=_cparams("parallel", "arbitrary"), name="retention",
    )(*args)


def _block_means_kernel(k_ref, o_ref, *, blk):
    for i in range(o_ref.shape[0]):
        o_ref[i:i + 1, :] = jnp.mean(k_ref[i * blk:(i + 1) * blk, :], axis=0, keepdims=True)


def _block_means(k2d, blk):
    m, w = k2d.shape
    n_blocks = m // blk
    per_step = 8 if n_blocks % 8 == 0 else n_blocks
    return pl.pallas_call(
        functools.partial(_block_means_kernel, blk=blk),
        grid=(n_blocks // per_step,),
        in_specs=[pl.BlockSpec((per_step * blk, w), lambda i: (i, 0))],
        out_specs=pl.BlockSpec((per_step, w), lambda i: (i, 0)),
        out_shape=jax.ShapeDtypeStruct((n_blocks, w), F32),
        compiler_params=_cparams("parallel"), name="block_means",
    )(k2d)


def _moba_prompt_kernel(q_ref, k_ref, v_ref, mean_ref, slope_ref, o_ref,
                        vt_ref, sel_ref, m_ref, l_ref, acc_ref, *, blk, topk, scale, grp):
    i = pl.program_id(2)
    nb = k_ref.shape[0] // blk
    inv_scale = 1.0 / scale
    exp2_coeff = scale * math.log2(math.e)

    @pl.when(i == 0)
    def _():
        for jb in range(nb):
            vt_ref[jb] = v_ref[jb * blk:(jb + 1) * blk, :].astype(F32).T.astype(BF16)

    slope_u = slope_ref[0][:, :1] * inv_scale
    qt = q_ref[...].astype(F32).T.astype(BF16)

    gate = jnp.dot(mean_ref[...].astype(BF16), qt, preferred_element_type=F32)
    bidx = lax.broadcasted_iota(jnp.int32, (nb, blk), 0)
    past = bidx < i
    gm = jnp.where(past, gate, -jnp.inf)
    cnt = jnp.zeros((nb, blk), jnp.int32)
    for jp in range(nb):
        row = gm[jp:jp + 1, :]
        beats = (row > gm) | ((row == gm) & (jp < bidx))
        cnt = cnt + beats.astype(jnp.int32)
    blk_dist = ((i - bidx) * blk).astype(F32)
    sel_ref[...] = jnp.where(past & (cnt < topk), 0.0, MASKED) - blk_dist * slope_u

    key_i = lax.broadcasted_iota(jnp.int32, (blk, blk), 0)
    qry_i = lax.broadcasted_iota(jnp.int32, (blk, blk), 1)
    dq = (qry_i - key_i).astype(F32)
    dbias = dq * slope_u

    k_own = k_ref[pl.ds(pl.multiple_of(i * blk, blk), blk), :]
    u = jnp.dot(k_own, qt, preferred_element_type=F32) - dbias
    u = jnp.where(dq >= 0, u, MASKED)
    m0 = jnp.max(u, axis=0, keepdims=True)
    p = jnp.exp2((u - m0) * exp2_coeff)
    l0 = jnp.sum(p, axis=0, keepdims=True)
    acc0 = jnp.dot(vt_ref[i], p.astype(BF16), preferred_element_type=F32)

    def group_update(j0, m_old, l_old, acc_old):
        us = []
        for g in range(grp):
            kj = k_ref[pl.ds(pl.multiple_of((j0 + g) * blk, blk), blk), :]
            ug = jnp.dot(kj, qt, preferred_element_type=F32)
            us.append(ug - dbias + sel_ref[pl.ds(j0 + g, 1), :])
        m_blk = us[0]
        for g in range(1, grp):
            m_blk = jnp.maximum(m_blk, us[g])
        m_new = jnp.maximum(m_old, jnp.max(m_blk, axis=0, keepdims=True))
        alpha = jnp.exp2((m_old - m_new) * exp2_coeff)
        acc = alpha * acc_old
        l_el = None
        for g in range(grp):
            pg = jnp.exp2((us[g] - m_new) * exp2_coeff)
            l_el = pg if l_el is None else l_el + pg
            acc = acc + jnp.dot(vt_ref[j0 + g], pg.astype(BF16), preferred_element_type=F32)
        return m_new, alpha * l_old + jnp.sum(l_el, axis=0, keepdims=True), acc

    def pair_body(pi, carry):
        carry = group_update(2 * pi * grp, *carry)
        return group_update((2 * pi + 1) * grp, *carry)

    n_grp = (i + grp - 1) // grp
    m_ref[...], l_ref[...], acc_ref[...] = lax.fori_loop(0, n_grp // 2, pair_body, (m0, l0, acc0))

    @pl.when(n_grp % 2 == 1)
    def _():
        m_ref[...], l_ref[...], acc_ref[...] = group_update(
            (n_grp - 1) * grp, m_ref[...], l_ref[...], acc_ref[...])

    o_ref[...] = (acc_ref[...] / l_ref[...]).T.astype(o_ref.dtype)


def _moba_prompt(q, k, v, means, slopes, *, n_seq):
    m, w = q.shape
    seq = m // n_seq
    dh = w // ATT_HEADS
    blk = MOBA_BLOCK
    nb = seq // blk
    return pl.pallas_call(
        functools.partial(_moba_prompt_kernel, blk=blk, topk=MOBA_TOPK, scale=dh ** -0.5,
                          grp=MOBA_KV_GROUP if nb % MOBA_KV_GROUP == 0 else 1),
        grid=(n_seq, ATT_HEADS, nb),
        in_specs=[pl.BlockSpec((blk, dh), lambda b, h, i: (b * nb + i, h)),
                  pl.BlockSpec((seq, dh), lambda b, h, i: (b, h)),
                  pl.BlockSpec((seq, dh), lambda b, h, i: (b, h)),
                  pl.BlockSpec((nb, dh), lambda b, h, i: (b, h)),
                  pl.BlockSpec((1, 1, LANES), lambda b, h, i: (h, 0, 0))],
        out_specs=pl.BlockSpec((blk, dh), lambda b, h, i: (b * nb + i, h)),
        out_shape=jax.ShapeDtypeStruct((m, w), BF16),
        scratch_shapes=[pltpu.VMEM((nb, dh, blk), BF16),
                        pltpu.VMEM((nb, blk), F32),
                        pltpu.VMEM((1, blk), F32),
                        pltpu.VMEM((1, blk), F32),
                        pltpu.VMEM((dh, blk), F32)],
        compiler_params=_cparams("arbitrary", "arbitrary", "arbitrary"), name="moba_prompt",
    )(q, k, v, means, slopes)


def _cache_means_kernel(pt_ref, *refs, ppb):
    del pt_ref
    o_ref = refs[-1]
    n_grp = (len(refs) - 1) // ppb
    for gi in range(n_grp):
        acc = jnp.sum(refs[gi * ppb][0], axis=0)
        for p in range(1, ppb):
            acc = acc + jnp.sum(refs[gi * ppb + p][0], axis=0)
        o_ref[0, gi] = acc * (1.0 / (ppb * refs[0].shape[1]))


def _cache_means(cache_k, page_table, n_full, ppb):
    n_seq = page_table.shape[0]
    _, ps, heads, dh = cache_k.shape
    grp = 4 if n_full % 4 == 0 else 1

    def page_map(b, n, pt, *, gi, p):
        return (pt[b, (n * grp + gi) * ppb + p], 0, 0, 0)

    in_specs = [pl.BlockSpec((1, ps, heads, dh), functools.partial(page_map, gi=gi, p=p))
                for gi in range(grp) for p in range(ppb)]
    return pl.pallas_call(
        functools.partial(_cache_means_kernel, ppb=ppb),
        grid_spec=pltpu.PrefetchScalarGridSpec(
            num_scalar_prefetch=1, grid=(n_seq, n_full // grp), in_specs=in_specs,
            out_specs=pl.BlockSpec((1, grp, heads, dh), lambda b, n, pt: (b, n, 0, 0))),
        out_shape=jax.ShapeDtypeStruct((n_seq, n_full, heads, dh), F32),
        compiler_params=_cparams("parallel", "arbitrary"), name="cache_means",
    )(page_table, *([cache_k] * (grp * ppb)))


def _sample_topk_kernel(q_ref, mean_ref, o_ref, *, heads, topk):
    t_n = q_ref.shape[1]
    dh = q_ref.shape[2] // heads
    nb = mean_ref.shape[1]
    lane = lax.broadcasted_iota(jnp.int32, (t_n, LANES), 1)
    bidx = lax.broadcasted_iota(jnp.int32, (t_n, nb), 1)
    out = jnp.zeros((t_n, LANES), jnp.int32)
    for h in range(heads):
        qh = q_ref[0, :, h * dh:(h + 1) * dh]
        mh = mean_ref[0, :, h * dh:(h + 1) * dh]
        gate = lax.dot_general(qh, mh, NT_DIMS, preferred_element_type=F32)
        cnt = jnp.zeros((t_n, nb), jnp.int32)
        for jp in range(nb):
            colv = gate[:, jp:jp + 1]
            beats = (colv > gate) | ((colv == gate) & (jp < bidx))
            cnt = cnt + beats.astype(jnp.int32)
        for r in range(topk):
            idx_r = jnp.sum(jnp.where(cnt == r, bidx, 0), axis=1, keepdims=True)
            out = jnp.where(lane == h * topk + r, idx_r, out)
    o_ref[0] = out


def _sample_topk(q3, means3):
    n_seq, t_n, w = q3.shape
    nb = means3.shape[1]
    return pl.pallas_call(
        functools.partial(_sample_topk_kernel, heads=ATT_HEADS, topk=MOBA_TOPK),
        grid=(n_seq,),
        in_specs=[pl.BlockSpec((1, t_n, w), lambda b: (b, 0, 0)),
                  pl.BlockSpec((1, nb, w), lambda b: (b, 0, 0))],
        out_specs=pl.BlockSpec((1, t_n, LANES), lambda b: (b, 0, 0)),
        out_shape=jax.ShapeDtypeStruct((n_seq, t_n, LANES), jnp.int32),
        compiler_params=_cparams("parallel"), name="sample_topk",
    )(q3, means3)


def _moba_sample_kernel(idx_ref, pt_ref, q_ref, kn_ref, vn_ref, slope_ref, *refs,
                        heads, topk, ppb, past_len, blk, scale):
    del pt_ref
    t_n = q_ref.shape[1]
    n_slots = t_n * topk * ppb
    k_refs, v_refs, o_ref = refs[:n_slots], refs[n_slots:2 * n_slots], refs[2 * n_slots]
    ps, dh = k_refs[0].shape[0], k_refs[0].shape[2]
    b = pl.program_id(0)
    h = pl.program_id(1)
    q = q_ref[0]
    slope = slope_ref[0][:, :1]
    row_i = lax.broadcasted_iota(jnp.int32, (t_n, ps), 0)
    lane_i = lax.broadcasted_iota(jnp.int32, (t_n, ps), 1)

    scores = []
    for t in range(t_n):
        for r in range(topk):
            blk_idx = idx_ref[((b * t_n + t) * heads + h) * topk + r]
            for p in range(ppb):
                kp = k_refs[(t * topk + r) * ppb + p].reshape(ps, dh)[...]
                s = lax.dot_general(q, kp, NT_DIMS, preferred_element_type=F32) * scale
                dist0 = past_len + t - blk_idx * blk - p * ps
                s = s - slope * (dist0 - lane_i).astype(F32)
                scores.append(jnp.where(row_i == t, s, MASKED))

    ro = lax.broadcasted_iota(jnp.int32, (t_n, t_n), 0)
    co = lax.broadcasted_iota(jnp.int32, (t_n, t_n), 1)
    s_own = lax.dot_general(q, kn_ref[0], NT_DIMS, preferred_element_type=F32) * scale
    s_own = s_own - slope * (ro - co).astype(F32)
    s_own = jnp.where(co <= ro, s_own, MASKED)

    m_el = scores[0]
    for s in scores[1:]:
        m_el = jnp.maximum(m_el, s)
    m = jnp.maximum(jnp.max(m_el, axis=1, keepdims=True), jnp.max(s_own, axis=1, keepdims=True))
    p_own = jnp.exp(s_own - m)
    acc = jnp.dot(p_own, vn_ref[0], preferred_element_type=F32)
    l_el = jnp.zeros((t_n, ps), F32)
    for slot, s in enumerate(scores):
        pe = jnp.exp(s - m)
        l_el = l_el + pe
        acc = acc + jnp.dot(pe, v_refs[slot].reshape(ps, dh)[...], preferred_element_type=F32)
    l = jnp.sum(l_el, axis=1, keepdims=True) + jnp.sum(p_own, axis=1, keepdims=True)
    o_ref[0] = acc / l


def _moba_sample(q3, kn3, vn3, cache_k, cache_v, page_table, idx_flat, slopes, *, past_len):
    n_seq, t_n, w = q3.shape
    n_pool, ps, heads, dh = cache_k.shape
    ppb = MOBA_BLOCK // ps
    ck = cache_k.reshape(n_pool, ps, heads, 1, dh)
    cv = cache_v.reshape(n_pool, ps, heads, 1, dh)

    def page_map(b, h, idx, pt, *, t, r, p):
        blk_idx = idx[((b * t_n + t) * heads + h) * MOBA_TOPK + r]
        return (pt[b, blk_idx * ppb + p], 0, h, 0, 0)

    slot_specs = [pl.BlockSpec((None, ps, None, 1, dh), functools.partial(page_map, t=t, r=r, p=p))
                  for t in range(t_n) for r in range(MOBA_TOPK) for p in range(ppb)]
    tok_spec = pl.BlockSpec((1, t_n, dh), lambda b, h, idx, pt: (b, 0, h))
    n_slots = len(slot_specs)
    return pl.pallas_call(
        functools.partial(_moba_sample_kernel, heads=heads, topk=MOBA_TOPK, ppb=ppb, past_len=past_len,
                          blk=MOBA_BLOCK, scale=dh ** -0.5),
        grid_spec=pltpu.PrefetchScalarGridSpec(
            num_scalar_prefetch=2, grid=(n_seq, heads),
            in_specs=[tok_spec, tok_spec, tok_spec,
                      pl.BlockSpec((1, 1, LANES), lambda b, h, idx, pt: (h, 0, 0))] + slot_specs + slot_specs,
            out_specs=tok_spec),
        out_shape=jax.ShapeDtypeStruct((n_seq, t_n, w), F32),
        compiler_params=_cparams("parallel", "arbitrary"), name="moba_sample",
    )(idx_flat, page_table, q3, kn3, vn3, slopes, *([ck] * n_slots), *([cv] * n_slots))


def _alibi_slope_rows():
    s = jnp.float32(2.0) ** (-8.0 * (jnp.arange(ATT_HEADS, dtype=F32) + 1.0) / ATT_HEADS)
    return jnp.broadcast_to(s[:, None, None], (ATT_HEADS, 1, LANES))


def _decoder(x, retention_fn, attention_fn, w, depth, is_prompt):
    n_seq, seq, dm = x.shape
    m = n_seq * seq
    n_ret = depth // 2
    alpha = (2 * depth) ** 0.25
    hq = w["ret_in"][0].shape[1] // 6
    dk = hq // RET_HEADS
    aw = w["kv"].shape[1] // 2
    h = x.reshape(m, dm)
    states = []
    k_f32 = v_f32 = k_b = v_b = None
    for layer in range(depth):
        if layer < n_ret:
            segs = ((0, hq, 1.0, (BF16,)), (hq, hq, dk ** -0.5, (BF16,)),
                    (2 * hq, 2 * hq, 1.0, (BF16,)), (4 * hq, 2 * hq, 1.0, (BF16,)))
            q, k, v, g = _proj(h, w["ret_in"][layer], segs)
            mix_in, s_new = retention_fn(layer, q, k, v, g)
            states.append(s_new)
            w_mix = w["ret_out"][layer]
        else:
            j = layer - n_ret
            if j == 0:
                k_dt = (F32, BF16) if is_prompt else (F32,)
                q_dt = (BF16,) if is_prompt else (F32,)
                segs = ((0, aw, 1.0, k_dt), (aw, aw, 1.0, k_dt), (2 * aw, aw, 1.0, q_dt))
                outs = _proj(h, jnp.concatenate([w["kv"], w["q"][j]], axis=1), segs)
                if is_prompt:
                    k_f32, k_b, v_f32, v_b, qa = outs
                else:
                    k_f32, v_f32, qa = outs
            else:
                (qa,) = _proj(h, w["q"][j], ((0, aw, 1.0, (BF16,) if is_prompt else (F32,)),))
            mix_in = attention_fn(qa, k_f32, v_f32, k_b, v_b)
            w_mix = w["att_out"][j]
        h = _outproj_ln(mix_in, w_mix, h, w["ln_g"][layer, 0], w["ln_b"][layer, 0], alpha)
        f = _ffn_in(h, w["ffn_in"][layer])
        h = _outproj_ln(f, w["ffn_out"][layer], h, w["ln_g"][layer, 1], w["ln_b"][layer, 1], alpha)
    return h.reshape(n_seq, seq, dm), jnp.stack(states), k_f32, v_f32


def kernel(x_prompt, x_sample, cache_k, cache_v, state_ret, page_table, w_ret_in, w_ret_out, w_kv,
           w_q, w_att_out, w_ffn_in, w_ffn_out, ln_g, ln_b):
    depth = w_ffn_in.shape[0]
    n_b, seq, _ = x_prompt.shape
    n_d, t_n, _ = x_sample.shape
    ps = cache_k.shape[1]
    n_pages = page_table.shape[1]
    past_len = n_pages * ps
    ppb = MOBA_BLOCK // ps
    n_full = past_len // MOBA_BLOCK
    assert seq % MOBA_BLOCK == 0 and seq % RET_CHUNK == 0
    assert n_full * ppb == n_pages and n_full >= MOBA_TOPK
    assert t_n <= SAMPLE_CHUNK_PAD

    w = {"ret_in": w_ret_in.astype(BF16), "ret_out": w_ret_out.astype(BF16), "kv": w_kv.astype(BF16),
         "q": w_q.astype(BF16), "att_out": w_att_out.astype(BF16), "ffn_in": w_ffn_in.astype(BF16),
         "ffn_out": w_ffn_out.astype(BF16), "ln_g": ln_g, "ln_b": ln_b}
    slopes = _alibi_slope_rows()

    def ret_prompt(layer, q, k, v, g):
        del layer
        c3 = lambda a: a.reshape(-1, RET_CHUNK, a.shape[-1])
        o, s = _retention(c3(q), c3(k), c3(v), c3(g), None, n_seq=n_b, c_real=RET_CHUNK)
        return o.reshape(n_b * seq, -1), s

    def att_prompt(q, k_f32, v_f32, k_b, v_b):
        del v_f32
        means = _block_means(k_f32, MOBA_BLOCK)
        return _moba_prompt(q, k_b, v_b, means, slopes, n_seq=n_b)

    y_p, s_p, k_p, v_p = _decoder(x_prompt, ret_prompt, att_prompt, w, depth, True)

    def ret_sample(layer, q, k, v, g):
        def c3(a):
            a = a.reshape(n_d, t_n, a.shape[-1])
            return jnp.pad(a, ((0, 0), (0, SAMPLE_CHUNK_PAD - t_n), (0, 0)))
        o, s = _retention(c3(q), c3(k), c3(v), c3(g), state_ret[layer].astype(F32), n_seq=n_d, c_real=t_n)
        return o[:, :t_n].reshape(n_d * t_n, -1), s

    def att_sample(q, k_f32, v_f32, k_b, v_b):
        del k_b, v_b
        r3 = lambda a: a.reshape(n_d, t_n, a.shape[-1])
        means = _cache_means(cache_k, page_table, n_full, ppb)
        idx = _sample_topk(r3(q), means.reshape(n_d, n_full, -1))
        idx_flat = idx[:, :, :ATT_HEADS * MOBA_TOPK].reshape(-1)
        o = _moba_sample(r3(q), r3(k_f32), r3(v_f32), cache_k, cache_v, page_table, idx_flat, slopes,
                         past_len=past_len)
        return o.reshape(n_d * t_n, -1).astype(BF16)

    y_s, s_s, k_s, v_s = _decoder(x_sample, ret_sample, att_sample, w, depth, False)

    dh = k_p.shape[-1] // ATT_HEADS
    kv4 = lambda a, n, l: a.reshape(n, l, ATT_HEADS, dh)
    return (y_p, y_s, s_p, s_s, kv4(k_p, n_b, seq), kv4(v_p, n_b, seq), kv4(k_s, n_d, t_n), kv4(v_s, n_d, t_n))
```

```python
import functools
import math

import jax
import jax.numpy as jnp
from jax import lax
from jax.experimental import pallas as pl
from jax.experimental.pallas import tpu as pltpu

F32 = jnp.float32
BF16 = jnp.bfloat16

RET_HEADS = 4
RET_CHUNK = 128
ATT_HEADS = 8
MOBA_BLOCK = 256
MOBA_TOPK = 3
PAGE_SIZE = 128
LN_EPS = 1e-5
GN_EPS = 1e-6
MASKED = -1e30
V7X_VMEM_LIMIT_BYTES = 56 * 1024 * 1024
SAMPLE_CHUNK_PAD = 16
LANES = 128
MOBA_KV_GROUP = 4
MOBA_HEADS_PER_STEP = 2

NT_DIMS = (((1,), (1,)), ((), ()))
TN_DIMS = (((0,), (0,)), ((), ()))


def _cparams(*sem):
    return pltpu.CompilerParams(dimension_semantics=sem, vmem_limit_bytes=V7X_VMEM_LIMIT_BYTES)


def _row_tile(m):
    for t in (512, 256, 128, 64, 32, 16, 8):
        if m % t == 0:
            return t
    return m


def _mm(a, b, dims=None, *, precise=False):
    dims = dims or (((a.ndim - 1,), (0,)), ((), ()))
    if precise:
        return lax.dot_general(a.astype(F32), b.astype(F32), dims, precision=lax.Precision.HIGHEST,
                               preferred_element_type=F32)
    return lax.dot_general(a.astype(BF16), b.astype(BF16), dims, preferred_element_type=F32)


def _proj_kernel(x_ref, w_ref, *o_refs, segs, chunk, precise):
    xb = x_ref[...] if precise else x_ref[...].astype(BF16)
    oi = 0
    for start, width, scale, dtypes in segs:
        for c0 in range(0, width, chunk):
            cw = min(chunk, width - c0)
            acc = _mm(xb, w_ref[:, start + c0:start + c0 + cw], precise=precise)
            if scale != 1.0:
                acc = acc * scale
            for d, dt in enumerate(dtypes):
                o_refs[oi + d][:, c0:c0 + cw] = acc.astype(dt)
        oi += len(dtypes)


def _proj(x, w, segs, precise=False):
    m, kdim = x.shape
    tm = _row_tile(m)
    if precise:
        outs = []
        for start, width, scale, dtypes in segs:
            assert start % width == 0
            outs += pl.pallas_call(
                functools.partial(_proj_kernel, segs=((0, width, scale, dtypes),), chunk=512, precise=True),
                grid=(m // tm,),
                in_specs=[pl.BlockSpec((tm, kdim), lambda i: (i, 0)),
                          pl.BlockSpec((kdim, width), functools.partial(lambda i, c: (0, c), c=start // width))],
                out_specs=[pl.BlockSpec((tm, width), lambda i: (i, 0)) for _ in dtypes],
                out_shape=[jax.ShapeDtypeStruct((m, width), dt) for dt in dtypes],
                compiler_params=_cparams("parallel"), name="proj_precise",
            )(x, w)
        return outs
    out_shape, out_specs = [], []
    for _, width, _, dtypes in segs:
        for dt in dtypes:
            out_shape.append(jax.ShapeDtypeStruct((m, width), dt))
            out_specs.append(pl.BlockSpec((tm, width), lambda i: (i, 0)))
    return pl.pallas_call(
        functools.partial(_proj_kernel, segs=segs, chunk=512, precise=False),
        grid=(m // tm,),
        in_specs=[pl.BlockSpec((tm, kdim), lambda i: (i, 0)),
                  pl.BlockSpec(w.shape, lambda i: (0, 0))],
        out_specs=out_specs, out_shape=out_shape,
        compiler_params=_cparams("parallel"), name="proj",
    )(x, w)


def _outproj_ln_kernel(a_ref, w_ref, r_ref, g_ref, b_ref, o_ref, *, alpha, precise):
    acc = _mm(a_ref[...], w_ref[...], precise=precise)
    y = alpha * r_ref[...] + acc
    mu = jnp.mean(y, axis=-1, keepdims=True)
    d = y - mu
    var = jnp.mean(d * d, axis=-1, keepdims=True)
    o_ref[...] = d * lax.rsqrt(var + LN_EPS) * g_ref[...] + b_ref[...]


def _outproj_ln(a, w, res, g, b, alpha, precise=False):
    m, kdim = a.shape
    dm = w.shape[1]
    tm = _row_tile(m)
    return pl.pallas_call(
        functools.partial(_outproj_ln_kernel, alpha=alpha, precise=precise),
        grid=(m // tm,),
        in_specs=[pl.BlockSpec((tm, kdim), lambda i: (i, 0)),
                  pl.BlockSpec((kdim, dm), lambda i: (0, 0)),
                  pl.BlockSpec((tm, dm), lambda i: (i, 0)),
                  pl.BlockSpec((1, dm), lambda i: (0, 0)),
                  pl.BlockSpec((1, dm), lambda i: (0, 0))],
        out_specs=pl.BlockSpec((tm, dm), lambda i: (i, 0)),
        out_shape=jax.ShapeDtypeStruct((m, dm), F32),
        compiler_params=_cparams("parallel"), name="outproj_ln",
    )(a, w, res, g.reshape(1, dm), b.reshape(1, dm))


def _ffn_in_kernel(h_ref, w_ref, o_ref, *, dff, chunk):
    hb = h_ref[...].astype(BF16)
    for c0 in range(0, dff, chunk):
        cw = min(chunk, dff - c0)
        gate = jnp.dot(hb, w_ref[:, c0:c0 + cw], preferred_element_type=F32)
        up = jnp.dot(hb, w_ref[:, dff + c0:dff + c0 + cw], preferred_element_type=F32)
        o_ref[:, c0:c0 + cw] = (gate * jax.nn.sigmoid(gate) * up).astype(o_ref.dtype)


def _ffn_in_chunk_kernel(h_ref, wg_ref, wu_ref, o_ref):
    gate = _mm(h_ref[...], wg_ref[...], precise=True)
    up = _mm(h_ref[...], wu_ref[...], precise=True)
    o_ref[...] = gate * jax.nn.sigmoid(gate) * up


def _ffn_in(h, w, precise=False):
    m, dm = h.shape
    dff = w.shape[1] // 2
    tm = _row_tile(m)
    if precise:
        tn = 256
        assert dff % tn == 0 and m == tm
        n_chunks = dff // tn
        return pl.pallas_call(
            _ffn_in_chunk_kernel,
            grid=(n_chunks,),
            in_specs=[pl.BlockSpec((m, dm), lambda c: (0, 0)),
                      pl.BlockSpec((dm, tn), lambda c: (0, c)),
                      pl.BlockSpec((dm, tn), lambda c: (0, n_chunks + c))],
            out_specs=pl.BlockSpec((m, tn), lambda c: (0, c)),
            out_shape=jax.ShapeDtypeStruct((m, dff), F32),
            compiler_params=_cparams("parallel"), name="ffn_in_precise",
        )(h, w, w)
    return pl.pallas_call(
        functools.partial(_ffn_in_kernel, dff=dff, chunk=512),
        grid=(m // tm,),
        in_specs=[pl.BlockSpec((tm, dm), lambda i: (i, 0)),
                  pl.BlockSpec(w.shape, lambda i: (0, 0))],
        out_specs=pl.BlockSpec((tm, dff), lambda i: (i, 0)),
        out_shape=jax.ShapeDtypeStruct((m, dff), BF16),
        compiler_params=_cparams("parallel"), name="ffn_in",
    )(h, w)


def _retention_kernel(*refs, heads, c_real, has_init, precise):
    if has_init:
        q_ref, k_ref, v_ref, g_ref, s0_ref, o_ref, s_ref = refs
    else:
        q_ref, k_ref, v_ref, g_ref, o_ref, s_ref = refs

    @pl.when(pl.program_id(1) == 0)
    def _():
        if has_init:
            s_ref[...] = s0_ref[...]
        else:
            s_ref[...] = jnp.zeros(s_ref.shape, F32)

    cp = q_ref.shape[1]
    dk = q_ref.shape[2] // heads
    dv = v_ref.shape[2] // heads
    row = lax.broadcasted_iota(jnp.int32, (cp, cp), 0)
    col = lax.broadcasted_iota(jnp.int32, (cp, cp), 1)
    diff = (row - col).astype(F32)
    ri = lax.broadcasted_iota(jnp.int32, (cp, 1), 0).astype(F32)
    for h in range(heads):
        lg = math.log(1.0 - 2.0 ** (-5.0 - h))
        decay = jnp.where(diff >= 0, jnp.exp(jnp.maximum(diff, 0.0) * lg), 0.0)
        xi = jnp.exp((ri + 1.0) * lg)
        zeta = jnp.exp((c_real - 1.0 - ri) * lg)
        qh = q_ref[0, :, h * dk:(h + 1) * dk]
        kh = k_ref[0, :, h * dk:(h + 1) * dk]
        vh = v_ref[0, :, h * dv:(h + 1) * dv]
        s_old = s_ref[0, h]
        scores = _mm(qh, kh, NT_DIMS, precise=precise) * decay
        intra = _mm(scores, vh, precise=precise)
        cross = _mm(qh, s_old, precise=precise) * xi
        upd = _mm(kh.astype(F32) * zeta, vh, TN_DIMS, precise=precise)
        s_ref[0, h] = math.exp(c_real * lg) * s_old + upd
        o = intra + cross
        mu = jnp.mean(o, axis=-1, keepdims=True)
        d = o - mu
        var = jnp.mean(d * d, axis=-1, keepdims=True)
        on = d * lax.rsqrt(var + GN_EPS)
        gh = g_ref[0, :, h * dv:(h + 1) * dv].astype(F32)
        o_ref[0, :, h * dv:(h + 1) * dv] = (gh * jax.nn.sigmoid(gh) * on).astype(o_ref.dtype)


def _retention(q, k, v, g, s0, *, n_seq, c_real, precise=False):
    n_chunks, cp, qw = q.shape
    vw = v.shape[2]
    nc = n_chunks // n_seq
    dk, dv = qw // RET_HEADS, vw // RET_HEADS
    tok_map = lambda b, c: (b * nc + c, 0, 0)
    state_spec = pl.BlockSpec((1, RET_HEADS, dk, dv), lambda b, c: (b, 0, 0, 0))
    in_specs = [pl.BlockSpec((1, cp, qw), tok_map), pl.BlockSpec((1, cp, qw), tok_map),
                pl.BlockSpec((1, cp, vw), tok_map), pl.BlockSpec((1, cp, vw), tok_map)]
    args = [q, k, v, g]
    if s0 is not None:
        in_specs.append(state_spec)
        args.append(s0)
    return pl.pallas_call(
        functools.partial(_retention_kernel, heads=RET_HEADS, c_real=c_real, has_init=s0 is not None,
                          precise=precise),
        grid=(n_seq, nc),
        in_specs=in_specs,
        out_specs=[pl.BlockSpec((1, cp, vw), tok_map), state_spec],
        out_shape=[jax.ShapeDtypeStruct((n_chunks, cp, vw), F32 if precise else BF16),
                   jax.ShapeDtypeStruct((n_seq, RET_HEADS, dk, dv), F32)],
        compiler_params=_cparams("parallel", "arbitrary"), name="retention",
    )(*args)


def _page_block_means(page_refs, o_ref, ppb):
    for gi in range(len(page_refs) // ppb):
        acc = jnp.sum(page_refs[gi * ppb][0], axis=0)
        for p in range(1, ppb):
            acc = acc + jnp.sum(page_refs[gi * ppb + p][0], axis=0)
        o_ref[0, gi] = acc * (1.0 / (ppb * page_refs[0].shape[1]))


def _cache_means_kernel(pt_ref, *refs, ppb):
    del pt_ref
    _page_block_means(refs[:-1], refs[-1], ppb)


def _cache_means(cache_k, page_table, n_full, ppb):
    n_seq = page_table.shape[0]
    _, ps, heads, dh = cache_k.shape
    grp = 4 if n_full % 4 == 0 else 1

    def page_map(b, n, pt, *, gi, p):
        return (pt[b, (n * grp + gi) * ppb + p], 0, 0, 0)

    in_specs = [pl.BlockSpec((1, ps, heads, dh), functools.partial(page_map, gi=gi, p=p))
                for gi in range(grp) for p in range(ppb)]
    return pl.pallas_call(
        functools.partial(_cache_means_kernel, ppb=ppb),
        grid_spec=pltpu.PrefetchScalarGridSpec(
            num_scalar_prefetch=1, grid=(n_seq, n_full // grp), in_specs=in_specs,
            out_specs=pl.BlockSpec((1, grp, heads, dh), lambda b, n, pt: (b, n, 0, 0))),
        out_shape=jax.ShapeDtypeStruct((n_seq, n_full, heads, dh), F32),
        compiler_params=_cparams("parallel", "arbitrary"), name="cache_means",
    )(page_table, *([cache_k] * (grp * ppb)))


def _block_means_kernel(k_ref, o_ref, *, blk):
    for i in range(o_ref.shape[0]):
        o_ref[i:i + 1, :] = jnp.mean(k_ref[i * blk:(i + 1) * blk, :], axis=0, keepdims=True)


def _block_means(k2d, blk):
    m, w = k2d.shape
    n_blocks = m // blk
    per_step = 8 if n_blocks % 8 == 0 else n_blocks
    return pl.pallas_call(
        functools.partial(_block_means_kernel, blk=blk),
        grid=(n_blocks // per_step,),
        in_specs=[pl.BlockSpec((per_step * blk, w), lambda i: (i, 0))],
        out_specs=pl.BlockSpec((per_step, w), lambda i: (i, 0)),
        out_shape=jax.ShapeDtypeStruct((n_blocks, w), F32),
        compiler_params=_cparams("parallel"), name="block_means",
    )(k2d)


def _rank_rows(g):
    n = g.shape[0]
    sub = 8
    sub_i = lax.broadcasted_iota(jnp.int32, (sub, g.shape[1]), 0)
    groups = []
    for r0 in range(0, n, sub):
        gr = g[r0:r0 + sub, :]
        cnt = jnp.zeros(gr.shape, jnp.int32)
        for jp in range(n):
            row = g[jp:jp + 1, :]
            if jp < r0:
                inc = jnp.where(row >= gr, 1, 0)
            elif jp >= r0 + gr.shape[0] - 1:
                inc = jnp.where(row > gr, 1, 0)
            else:
                inc = jnp.where(sub_i[:gr.shape[0]] > jp - r0,
                                jnp.where(row >= gr, 1, 0), jnp.where(row > gr, 1, 0))
            cnt = cnt + inc
        groups.append(cnt)
    return jnp.concatenate(groups, axis=0)


def _moba_prompt_kernel(*refs, blk, topk, scale, grp, hps, n_page_refs, ppb):
    if n_page_refs:
        refs = refs[1:]
    q_ref, k_ref, v_ref, mean_ref, slope_ref = refs[:5]
    page_refs = refs[5:5 + n_page_refs]
    refs = refs[5 + n_page_refs:]
    if n_page_refs:
        o_ref, cmean_ref = refs[:2]
        refs = refs[2:]
    else:
        o_ref = refs[0]
        refs = refs[1:]
    vt_ref, sel_ref, u_ref, mx_ref = refs

    i = pl.program_id(2)
    nb = k_ref.shape[0] // blk
    dh = q_ref.shape[1] // hps
    inv_scale = 1.0 / scale
    exp2_coeff = scale * math.log2(math.e)
    hcols = lambda hh: slice(hh * dh, (hh + 1) * dh)

    @pl.when(i == 0)
    def _():
        for hh in range(hps):
            for jb in range(nb):
                vt_ref[hh, jb] = v_ref[jb * blk:(jb + 1) * blk, hcols(hh)].astype(F32).T.astype(BF16)

    if n_page_refs:
        _page_block_means(page_refs, cmean_ref, ppb)

    key_i = lax.broadcasted_iota(jnp.int32, (blk, blk), 0)
    qry_i = lax.broadcasted_iota(jnp.int32, (blk, blk), 1)
    dq = (qry_i - key_i).astype(F32)
    bidx = lax.broadcasted_iota(jnp.int32, (nb, blk), 0)
    past = bidx < i
    blk_dist = ((i - bidx) * blk).astype(F32)

    qts, dbiases, state = [], [], []
    for hh in range(hps):
        slope_u = slope_ref[hh][:, :1] * inv_scale
        qt = q_ref[:, hcols(hh)].astype(F32).T.astype(BF16)
        gate = jnp.dot(mean_ref[:, hcols(hh)].astype(BF16), qt, preferred_element_type=F32)
        gm = jnp.where(past, gate, -jnp.inf)
        cnt = _rank_rows(gm)
        sel_ref[hh] = jnp.where(past & (cnt < topk), 0.0, MASKED) - blk_dist * slope_u
        dbias = dq * slope_u
        k_own = k_ref[pl.ds(pl.multiple_of(i * blk, blk), blk), hcols(hh)]
        u = jnp.dot(k_own, qt, preferred_element_type=F32) - dbias
        u = jnp.where(dq >= 0, u, MASKED)
        m0 = jnp.max(u, axis=0, keepdims=True)
        p = jnp.exp2((u - m0) * exp2_coeff)
        l0 = jnp.sum(p, axis=0, keepdims=True)
        acc0 = jnp.dot(vt_ref[hh, i], p.astype(BF16), preferred_element_type=F32)
        qts.append(qt)
        dbiases.append(dbias)
        state.append((m0, l0, acc0))

    def stage_a(hh, j0):
        mx = None
        for g in range(grp):
            kj = k_ref[pl.ds(pl.multiple_of((j0 + g) * blk, blk), blk), hcols(hh)]
            u = jnp.dot(kj, qts[hh], preferred_element_type=F32) - dbiases[hh] + sel_ref[hh, pl.ds(j0 + g, 1), :]
            u_ref[hh, g * blk:(g + 1) * blk, :] = u
            mx = u if mx is None else jnp.maximum(mx, u)
        mx_ref[hh] = jnp.max(mx, axis=0, keepdims=True)

    def stage_b(hh, j0, m_old, l_old, acc_old):
        m_new = jnp.maximum(m_old, mx_ref[hh])
        alpha = jnp.exp2((m_old - m_new) * exp2_coeff)
        acc = alpha * acc_old
        l_el = None
        for g in range(grp):
            pg = jnp.exp2((u_ref[hh, g * blk:(g + 1) * blk, :] - m_new) * exp2_coeff)
            l_el = pg if l_el is None else l_el + pg
            acc = acc + jnp.dot(vt_ref[hh, j0 + g], pg.astype(BF16), preferred_element_type=F32)
        return m_new, alpha * l_old + jnp.sum(l_el, axis=0, keepdims=True), acc

    n_grp = jnp.maximum((i + grp - 1) // grp, 1)
    for hh in range(hps):
        stage_a(hh, 0)

    def trip(gi, carry):
        out = tuple(stage_b(hh, gi * grp, *carry[hh]) for hh in range(hps))
        for hh in range(hps):
            stage_a(hh, (gi + 1) * grp)
        return out

    state = lax.fori_loop(0, n_grp - 1, trip, tuple(state))
    for hh in range(hps):
        _, l_f, acc_f = stage_b(hh, (n_grp - 1) * grp, *state[hh])
        o_ref[:, hcols(hh)] = (acc_f / l_f).T.astype(o_ref.dtype)


def _moba_prompt(q, k, v, means, slopes, *, n_seq, cache_k=None, page_table=None):
    m, w = q.shape
    seq = m // n_seq
    dh = w // ATT_HEADS
    blk = MOBA_BLOCK
    nb = seq // blk
    hps = MOBA_HEADS_PER_STEP if ATT_HEADS % MOBA_HEADS_PER_STEP == 0 else 1
    grp = MOBA_KV_GROUP if nb % MOBA_KV_GROUP == 0 else 1
    n_hp = ATT_HEADS // hps
    n_steps = n_seq * n_hp * nb

    bps = 0
    if cache_k is not None:
        n_dec, n_pages = page_table.shape
        _, ps, c_heads, c_dh = cache_k.shape
        ppb = blk // ps
        n_full = n_pages // ppb
        if (n_dec * n_full) % n_steps == 0 and n_full % ((n_dec * n_full) // n_steps) == 0:
            bps = (n_dec * n_full) // n_steps
    fused = bps > 0

    def tok_map(b, h, i, *_):
        return (b * nb + i, h)

    def seq_map(b, h, i, *_):
        return (b, h)

    in_specs = [pl.BlockSpec((blk, hps * dh), tok_map),
                pl.BlockSpec((seq, hps * dh), seq_map),
                pl.BlockSpec((seq, hps * dh), seq_map),
                pl.BlockSpec((nb, hps * dh), seq_map),
                pl.BlockSpec((hps, 1, LANES), lambda b, h, i, *_: (h, 0, 0))]
    out_specs = [pl.BlockSpec((blk, hps * dh), tok_map)]
    out_shape = [jax.ShapeDtypeStruct((m, w), BF16)]
    args = [q, k, v, means, slopes]
    prefetch = []
    if fused:
        steps_per_seq = n_full // bps

        def step_of(b, h, i):
            return (b * n_hp + h) * nb + i

        def page_map(b, h, i, pt, *, n):
            s = step_of(b, h, i)
            return (pt[s // steps_per_seq, (s % steps_per_seq) * bps * ppb + n], 0, 0, 0)

        in_specs += [pl.BlockSpec((1, ps, c_heads, c_dh), functools.partial(page_map, n=n))
                     for n in range(bps * ppb)]
        out_specs.append(pl.BlockSpec(
            (1, bps, c_heads, c_dh),
            lambda b, h, i, pt: (step_of(b, h, i) // steps_per_seq, step_of(b, h, i) % steps_per_seq, 0, 0)))
        out_shape.append(jax.ShapeDtypeStruct((n_dec, n_full, c_heads, c_dh), F32))
        args += [cache_k] * (bps * ppb)
        prefetch = [page_table]

    outs = pl.pallas_call(
        functools.partial(_moba_prompt_kernel, blk=blk, topk=MOBA_TOPK, scale=dh ** -0.5, grp=grp, hps=hps,
                          n_page_refs=bps * ppb if fused else 0, ppb=ppb if fused else 1),
        grid_spec=pltpu.PrefetchScalarGridSpec(
            num_scalar_prefetch=len(prefetch), grid=(n_seq, n_hp, nb),
            in_specs=in_specs, out_specs=out_specs,
            scratch_shapes=[pltpu.VMEM((hps, nb, dh, blk), BF16),
                            pltpu.VMEM((hps, nb, blk), F32),
                            pltpu.VMEM((hps, grp * blk, blk), F32),
                            pltpu.VMEM((hps, 1, blk), F32)]),
        out_shape=out_shape,
        compiler_params=_cparams("arbitrary", "arbitrary", "arbitrary"), name="moba_prompt",
    )(*prefetch, *args)
    return outs[0], (outs[1] if fused else None)


def _sample_topk_kernel(q_ref, mean_ref, o_ref, *, heads, topk):
    t_n = q_ref.shape[1]
    dh = q_ref.shape[2] // heads
    nb = mean_ref.shape[1]
    lane = lax.broadcasted_iota(jnp.int32, (t_n, LANES), 1)
    bidx = lax.broadcasted_iota(jnp.int32, (t_n, nb), 1)
    out = jnp.zeros((t_n, LANES), jnp.int32)
    for h in range(heads):
        qh = q_ref[0, :, h * dh:(h + 1) * dh]
        mh = mean_ref[0, :, h * dh:(h + 1) * dh]
        gate = _mm(qh, mh, NT_DIMS, precise=True)
        cnt = jnp.zeros((t_n, nb), jnp.int32)
        for jp in range(nb):
            colv = gate[:, jp:jp + 1]
            beats = (colv > gate) | ((colv == gate) & (jp < bidx))
            cnt = cnt + beats.astype(jnp.int32)
        for r in range(topk):
            idx_r = jnp.sum(jnp.where(cnt == r, bidx, 0), axis=1, keepdims=True)
            out = jnp.where(lane == h * topk + r, idx_r, out)
    o_ref[0] = out


def _sample_topk(q3, means3):
    n_seq, t_n, w = q3.shape
    nb = means3.shape[1]
    return pl.pallas_call(
        functools.partial(_sample_topk_kernel, heads=ATT_HEADS, topk=MOBA_TOPK),
        grid=(n_seq,),
        in_specs=[pl.BlockSpec((1, t_n, w), lambda b: (b, 0, 0)),
                  pl.BlockSpec((1, nb, w), lambda b: (b, 0, 0))],
        out_specs=pl.BlockSpec((1, t_n, LANES), lambda b: (b, 0, 0)),
        out_shape=jax.ShapeDtypeStruct((n_seq, t_n, LANES), jnp.int32),
        compiler_params=_cparams("parallel"), name="sample_topk",
    )(q3, means3)


def _moba_sample_kernel(idx_ref, pt_ref, q_ref, kn_ref, vn_ref, slope_ref, *refs,
                        heads, topk, ppb, past_len, blk, scale):
    del pt_ref
    t_n = q_ref.shape[1]
    n_slots = t_n * topk * ppb
    k_refs, v_refs, o_ref = refs[:n_slots], refs[n_slots:2 * n_slots], refs[2 * n_slots]
    ps, dh = k_refs[0].shape[0], k_refs[0].shape[2]
    b = pl.program_id(0)
    h = pl.program_id(1)
    q = q_ref[0]
    slope = slope_ref[0][:, :1]
    row_i = lax.broadcasted_iota(jnp.int32, (t_n, ps), 0)
    lane_i = lax.broadcasted_iota(jnp.int32, (t_n, ps), 1)

    scores = []
    for t in range(t_n):
        for r in range(topk):
            blk_idx = idx_ref[((b * t_n + t) * heads + h) * topk + r]
            for p in range(ppb):
                kp = k_refs[(t * topk + r) * ppb + p].reshape(ps, dh)[...]
                s = _mm(q, kp, NT_DIMS, precise=True) * scale
                dist0 = past_len + t - blk_idx * blk - p * ps
                s = s - slope * (dist0 - lane_i).astype(F32)
                scores.append(jnp.where(row_i == t, s, MASKED))

    ro = lax.broadcasted_iota(jnp.int32, (t_n, t_n), 0)
    co = lax.broadcasted_iota(jnp.int32, (t_n, t_n), 1)
    s_own = _mm(q, kn_ref[0], NT_DIMS, precise=True) * scale
    s_own = s_own - slope * (ro - co).astype(F32)
    s_own = jnp.where(co <= ro, s_own, MASKED)

    m_el = scores[0]
    for s in scores[1:]:
        m_el = jnp.maximum(m_el, s)
    m = jnp.maximum(jnp.max(m_el, axis=1, keepdims=True), jnp.max(s_own, axis=1, keepdims=True))
    p_own = jnp.exp(s_own - m)
    acc = _mm(p_own, vn_ref[0], precise=True)
    l_el = jnp.zeros((t_n, ps), F32)
    for slot, s in enumerate(scores):
        pe = jnp.exp(s - m)
        l_el = l_el + pe
        acc = acc + _mm(pe, v_refs[slot].reshape(ps, dh)[...], precise=True)
    l = jnp.sum(l_el, axis=1, keepdims=True) + jnp.sum(p_own, axis=1, keepdims=True)
    o_ref[0] = acc / l


def _moba_sample(q3, kn3, vn3, cache_k, cache_v, page_table, idx_flat, slopes, *, past_len):
    n_seq, t_n, w = q3.shape
    n_pool, ps, heads, dh = cache_k.shape
    ppb = MOBA_BLOCK // ps
    ck = cache_k.reshape(n_pool, ps, heads, 1, dh)
    cv = cache_v.reshape(n_pool, ps, heads, 1, dh)

    def page_map(b, h, idx, pt, *, t, r, p):
        blk_idx = idx[((b * t_n + t) * heads + h) * MOBA_TOPK + r]
        return (pt[b, blk_idx * ppb + p], 0, h, 0, 0)

    slot_specs = [pl.BlockSpec((None, ps, None, 1, dh), functools.partial(page_map, t=t, r=r, p=p))
                  for t in range(t_n) for r in range(MOBA_TOPK) for p in range(ppb)]
    tok_spec = pl.BlockSpec((1, t_n, dh), lambda b, h, idx, pt: (b, 0, h))
    n_slots = len(slot_specs)
    return pl.pallas_call(
        functools.partial(_moba_sample_kernel, heads=heads, topk=MOBA_TOPK, ppb=ppb, past_len=past_len,
                          blk=MOBA_BLOCK, scale=dh ** -0.5),
        grid_spec=pltpu.PrefetchScalarGridSpec(
            num_scalar_prefetch=2, grid=(n_seq, heads),
            in_specs=[tok_spec, tok_spec, tok_spec,
                      pl.BlockSpec((1, 1, LANES), lambda b, h, idx, pt: (h, 0, 0))] + slot_specs + slot_specs,
            out_specs=tok_spec),
        out_shape=jax.ShapeDtypeStruct((n_seq, t_n, w), F32),
        compiler_params=_cparams("parallel", "arbitrary"), name="moba_sample",
    )(idx_flat, page_table, q3, kn3, vn3, slopes, *([ck] * n_slots), *([cv] * n_slots))


def _alibi_slope_rows():
    s = jnp.float32(2.0) ** (-8.0 * (jnp.arange(ATT_HEADS, dtype=F32) + 1.0) / ATT_HEADS)
    return jnp.broadcast_to(s[:, None, None], (ATT_HEADS, 1, LANES))


def _decoder(x, retention_fn, attention_fn, w, depth, is_prompt):
    n_seq, seq, dm = x.shape
    m = n_seq * seq
    n_ret = depth // 2
    alpha = (2 * depth) ** 0.25
    hq = w["ret_in"][0].shape[1] // 6
    dk = hq // RET_HEADS
    aw = w["kv"].shape[1] // 2
    h = x.reshape(m, dm)
    states = []
    k_f32 = v_f32 = k_b = v_b = None
    precise = not is_prompt
    act = F32 if precise else BF16
    for layer in range(depth):
        if layer < n_ret:
            segs = ((0, hq, 1.0, (act,)), (hq, hq, dk ** -0.5, (act,)),
                    (2 * hq, 2 * hq, 1.0, (act,)), (4 * hq, 2 * hq, 1.0, (act,)))
            q, k, v, g = _proj(h, w["ret_in"][layer], segs, precise)
            mix_in, s_new = retention_fn(layer, q, k, v, g)
            states.append(s_new)
            w_mix = w["ret_out"][layer]
        else:
            j = layer - n_ret
            if j == 0:
                k_dt = (F32, BF16) if is_prompt else (F32,)
                q_dt = (BF16,) if is_prompt else (F32,)
                segs = ((0, aw, 1.0, k_dt), (aw, aw, 1.0, k_dt), (2 * aw, aw, 1.0, q_dt))
                outs = _proj(h, jnp.concatenate([w["kv"], w["q"][j]], axis=1), segs, precise)
                if is_prompt:
                    k_f32, k_b, v_f32, v_b, qa = outs
                else:
                    k_f32, v_f32, qa = outs
            else:
                (qa,) = _proj(h, w["q"][j], ((0, aw, 1.0, (BF16,) if is_prompt else (F32,)),), precise)
            mix_in = attention_fn(j, qa, k_f32, v_f32, k_b, v_b)
            w_mix = w["att_out"][j]
        h = _outproj_ln(mix_in, w_mix, h, w["ln_g"][layer, 0], w["ln_b"][layer, 0], alpha, precise)
        f = _ffn_in(h, w["ffn_in"][layer], precise)
        h = _outproj_ln(f, w["ffn_out"][layer], h, w["ln_g"][layer, 1], w["ln_b"][layer, 1], alpha, precise)
    return h.reshape(n_seq, seq, dm), jnp.stack(states), k_f32, v_f32


def kernel(x_prompt, x_sample, cache_k, cache_v, state_ret, page_table, w_ret_in, w_ret_out, w_kv,
           w_q, w_att_out, w_ffn_in, w_ffn_out, ln_g, ln_b):
    depth = w_ffn_in.shape[0]
    n_b, seq, _ = x_prompt.shape
    n_d, t_n, _ = x_sample.shape
    ps = cache_k.shape[1]
    n_pages = page_table.shape[1]
    past_len = n_pages * ps
    ppb = MOBA_BLOCK // ps
    n_full = past_len // MOBA_BLOCK
    assert seq % MOBA_BLOCK == 0 and seq % RET_CHUNK == 0
    assert n_full * ppb == n_pages and n_full >= MOBA_TOPK
    assert t_n <= SAMPLE_CHUNK_PAD

    w32 = {"ret_in": w_ret_in, "ret_out": w_ret_out, "kv": w_kv, "q": w_q, "att_out": w_att_out,
           "ffn_in": w_ffn_in, "ffn_out": w_ffn_out, "ln_g": ln_g, "ln_b": ln_b}
    w = {name: (a if name.startswith("ln_") else a.astype(BF16)) for name, a in w32.items()}
    slopes = _alibi_slope_rows()
    cache_means = [None]

    def ret_prompt(layer, q, k, v, g):
        del layer
        c3 = lambda a: a.reshape(-1, RET_CHUNK, a.shape[-1])
        o, s = _retention(c3(q), c3(k), c3(v), c3(g), None, n_seq=n_b, c_real=RET_CHUNK)
        return o.reshape(n_b * seq, -1), s

    def att_prompt(j, q, k_f32, v_f32, k_b, v_b):
        del v_f32
        means = _block_means(k_f32, MOBA_BLOCK)
        if j == 0:
            o, cache_means[0] = _moba_prompt(q, k_b, v_b, means, slopes, n_seq=n_b,
                                             cache_k=cache_k, page_table=page_table)
        else:
            o, _ = _moba_prompt(q, k_b, v_b, means, slopes, n_seq=n_b)
        return o

    y_p, s_p, k_p, v_p = _decoder(x_prompt, ret_prompt, att_prompt, w, depth, True)

    def ret_sample(layer, q, k, v, g):
        def c3(a):
            a = a.reshape(n_d, t_n, a.shape[-1])
            return jnp.pad(a, ((0, 0), (0, SAMPLE_CHUNK_PAD - t_n), (0, 0)))
        o, s = _retention(c3(q), c3(k), c3(v), c3(g), state_ret[layer].astype(F32), n_seq=n_d, c_real=t_n,
                          precise=True)
        return o[:, :t_n].reshape(n_d * t_n, -1), s

    def att_sample(j, q, k_f32, v_f32, k_b, v_b):
        del j, k_b, v_b
        r3 = lambda a: a.reshape(n_d, t_n, a.shape[-1])
        if cache_means[0] is None:
            cache_means[0] = _cache_means(cache_k, page_table, n_full, ppb)
        idx = _sample_topk(r3(q), cache_means[0].reshape(n_d, n_full, -1))
        idx_flat = idx[:, :, :ATT_HEADS * MOBA_TOPK].reshape(-1)
        o = _moba_sample(r3(q), r3(k_f32), r3(v_f32), cache_k, cache_v, page_table, idx_flat, slopes,
                         past_len=past_len)
        return o.reshape(n_d * t_n, -1)

    y_s, s_s, k_s, v_s = _decoder(x_sample, ret_sample, att_sample, w32, depth, False)

    dh = k_p.shape[-1] // ATT_HEADS
    kv4 = lambda a, n, l: a.reshape(n, l, ATT_HEADS, dh)
    return (y_p, y_s, s_p, s_s, kv4(k_p, n_b, seq), kv4(v_p, n_b, seq), kv4(k_s, n_d, t_n), kv4(v_s, n_d, t_n))
```

```python
import functools
import math

import jax
import jax.numpy as jnp
from jax import lax
from jax.experimental import pallas as pl
from jax.experimental.pallas import tpu as pltpu

F32 = jnp.float32
BF16 = jnp.bfloat16

RET_HEADS = 4
RET_CHUNK = 128
ATT_HEADS = 8
MOBA_BLOCK = 256
MOBA_TOPK = 3
PAGE_SIZE = 128
LN_EPS = 1e-5
GN_EPS = 1e-6
MASKED = -1e30
V7X_VMEM_LIMIT_BYTES = 56 * 1024 * 1024
SAMPLE_CHUNK_PAD = 16
LANES = 128
MOBA_KV_GROUP = 4
MOBA_HEADS_PER_STEP = 2

NT_DIMS = (((1,), (1,)), ((), ()))
TN_DIMS = (((0,), (0,)), ((), ()))


def _cparams(*sem):
    return pltpu.CompilerParams(dimension_semantics=sem, vmem_limit_bytes=V7X_VMEM_LIMIT_BYTES)


def _row_tile(m):
    for t in (512, 256, 128, 64, 32, 16, 8):
        if m % t == 0:
            return t
    return m


def _mm(a, b, dims=None, *, precise=False):
    dims = dims or (((a.ndim - 1,), (0,)), ((), ()))
    if precise:
        return lax.dot_general(a.astype(F32), b.astype(F32), dims, precision=lax.Precision.HIGHEST,
                               preferred_element_type=F32)
    return lax.dot_general(a.astype(BF16), b.astype(BF16), dims, preferred_element_type=F32)


def _proj_kernel(x_ref, w_ref, *o_refs, segs, chunk, precise):
    xb = x_ref[...] if precise else x_ref[...].astype(BF16)
    oi = 0
    for start, width, scale, dtypes in segs:
        for c0 in range(0, width, chunk):
            cw = min(chunk, width - c0)
            acc = _mm(xb, w_ref[:, start + c0:start + c0 + cw], precise=precise)
            if scale != 1.0:
                acc = acc * scale
            for d, dt in enumerate(dtypes):
                o_refs[oi + d][:, c0:c0 + cw] = acc.astype(dt)
        oi += len(dtypes)


def _proj(x, w, segs, precise=False):
    m, kdim = x.shape
    tm = _row_tile(m)
    if precise:
        outs = []
        for start, width, scale, dtypes in segs:
            assert start % width == 0
            outs += pl.pallas_call(
                functools.partial(_proj_kernel, segs=((0, width, scale, dtypes),), chunk=512, precise=True),
                grid=(m // tm,),
                in_specs=[pl.BlockSpec((tm, kdim), lambda i: (i, 0)),
                          pl.BlockSpec((kdim, width), functools.partial(lambda i, c: (0, c), c=start // width))],
                out_specs=[pl.BlockSpec((tm, width), lambda i: (i, 0)) for _ in dtypes],
                out_shape=[jax.ShapeDtypeStruct((m, width), dt) for dt in dtypes],
                compiler_params=_cparams("parallel"), name="proj_precise",
            )(x, w)
        return outs
    out_shape, out_specs = [], []
    for _, width, _, dtypes in segs:
        for dt in dtypes:
            out_shape.append(jax.ShapeDtypeStruct((m, width), dt))
            out_specs.append(pl.BlockSpec((tm, width), lambda i: (i, 0)))
    return pl.pallas_call(
        functools.partial(_proj_kernel, segs=segs, chunk=512, precise=False),
        grid=(m // tm,),
        in_specs=[pl.BlockSpec((tm, kdim), lambda i: (i, 0)),
                  pl.BlockSpec(w.shape, lambda i: (0, 0))],
        out_specs=out_specs, out_shape=out_shape,
        compiler_params=_cparams("parallel"), name="proj",
    )(x, w)


def _outproj_ln_kernel(a_ref, w_ref, r_ref, g_ref, b_ref, o_ref, *, alpha, precise):
    acc = _mm(a_ref[...], w_ref[...], precise=precise)
    y = alpha * r_ref[...] + acc
    mu = jnp.mean(y, axis=-1, keepdims=True)
    d = y - mu
    var = jnp.mean(d * d, axis=-1, keepdims=True)
    o_ref[...] = d * lax.rsqrt(var + LN_EPS) * g_ref[...] + b_ref[...]


def _outproj_ln(a, w, res, g, b, alpha, precise=False):
    m, kdim = a.shape
    dm = w.shape[1]
    tm = _row_tile(m)
    return pl.pallas_call(
        functools.partial(_outproj_ln_kernel, alpha=alpha, precise=precise),
        grid=(m // tm,),
        in_specs=[pl.BlockSpec((tm, kdim), lambda i: (i, 0)),
                  pl.BlockSpec((kdim, dm), lambda i: (0, 0)),
                  pl.BlockSpec((tm, dm), lambda i: (i, 0)),
                  pl.BlockSpec((1, dm), lambda i: (0, 0)),
                  pl.BlockSpec((1, dm), lambda i: (0, 0))],
        out_specs=pl.BlockSpec((tm, dm), lambda i: (i, 0)),
        out_shape=jax.ShapeDtypeStruct((m, dm), F32),
        compiler_params=_cparams("parallel"), name="outproj_ln",
    )(a, w, res, g.reshape(1, dm), b.reshape(1, dm))


def _ffn_in_kernel(h_ref, w_ref, o_ref, *, dff, chunk):
    hb = h_ref[...].astype(BF16)
    for c0 in range(0, dff, chunk):
        cw = min(chunk, dff - c0)
        gate = jnp.dot(hb, w_ref[:, c0:c0 + cw], preferred_element_type=F32)
        up = jnp.dot(hb, w_ref[:, dff + c0:dff + c0 + cw], preferred_element_type=F32)
        o_ref[:, c0:c0 + cw] = (gate * jax.nn.sigmoid(gate) * up).astype(o_ref.dtype)


def _ffn_in_chunk_kernel(h_ref, wg_ref, wu_ref, o_ref):
    gate = _mm(h_ref[...], wg_ref[...], precise=True)
    up = _mm(h_ref[...], wu_ref[...], precise=True)
    o_ref[...] = gate * jax.nn.sigmoid(gate) * up


def _ffn_in(h, w, precise=False):
    m, dm = h.shape
    dff = w.shape[1] // 2
    tm = _row_tile(m)
    if precise:
        tn = 256
        assert dff % tn == 0 and m == tm
        n_chunks = dff // tn
        return pl.pallas_call(
            _ffn_in_chunk_kernel,
            grid=(n_chunks,),
            in_specs=[pl.BlockSpec((m, dm), lambda c: (0, 0)),
                      pl.BlockSpec((dm, tn), lambda c: (0, c)),
                      pl.BlockSpec((dm, tn), lambda c: (0, n_chunks + c))],
            out_specs=pl.BlockSpec((m, tn), lambda c: (0, c)),
            out_shape=jax.ShapeDtypeStruct((m, dff), F32),
            compiler_params=_cparams("parallel"), name="ffn_in_precise",
        )(h, w, w)
    return pl.pallas_call(
        functools.partial(_ffn_in_kernel, dff=dff, chunk=512),
        grid=(m // tm,),
        in_specs=[pl.BlockSpec((tm, dm), lambda i: (i, 0)),
                  pl.BlockSpec(w.shape, lambda i: (0, 0))],
        out_specs=pl.BlockSpec((tm, dff), lambda i: (i, 0)),
        out_shape=jax.ShapeDtypeStruct((m, dff), BF16),
        compiler_params=_cparams("parallel"), name="ffn_in",
    )(h, w)


def _retention_kernel(*refs, heads, c_real, has_init, precise):
    if has_init:
        q_ref, k_ref, v_ref, g_ref, s0_ref, o_ref, s_ref = refs
    else:
        q_ref, k_ref, v_ref, g_ref, o_ref, s_ref = refs

    @pl.when(pl.program_id(1) == 0)
    def _():
        if has_init:
            s_ref[...] = s0_ref[...]
        else:
            s_ref[...] = jnp.zeros(s_ref.shape, F32)

    cp = q_ref.shape[1]
    dk = q_ref.shape[2] // heads
    dv = v_ref.shape[2] // heads
    row = lax.broadcasted_iota(jnp.int32, (cp, cp), 0)
    col = lax.broadcasted_iota(jnp.int32, (cp, cp), 1)
    diff = (row - col).astype(F32)
    ri = lax.broadcasted_iota(jnp.int32, (cp, 1), 0).astype(F32)
    for h in range(heads):
        lg = math.log(1.0 - 2.0 ** (-5.0 - h))
        decay = jnp.where(diff >= 0, jnp.exp(jnp.maximum(diff, 0.0) * lg), 0.0)
        xi = jnp.exp((ri + 1.0) * lg)
        zeta = jnp.exp((c_real - 1.0 - ri) * lg)
        qh = q_ref[0, :, h * dk:(h + 1) * dk]
        kh = k_ref[0, :, h * dk:(h + 1) * dk]
        vh = v_ref[0, :, h * dv:(h + 1) * dv]
        s_old = s_ref[0, h]
        scores = _mm(qh, kh, NT_DIMS, precise=precise) * decay
        intra = _mm(scores, vh, precise=precise)
        cross = _mm(qh, s_old, precise=precise) * xi
        upd = _mm(kh.astype(F32) * zeta, vh, TN_DIMS, precise=precise)
        s_ref[0, h] = math.exp(c_real * lg) * s_old + upd
        o = intra + cross
        mu = jnp.mean(o, axis=-1, keepdims=True)
        d = o - mu
        var = jnp.mean(d * d, axis=-1, keepdims=True)
        on = d * lax.rsqrt(var + GN_EPS)
        gh = g_ref[0, :, h * dv:(h + 1) * dv].astype(F32)
        o_ref[0, :, h * dv:(h + 1) * dv] = (gh * jax.nn.sigmoid(gh) * on).astype(o_ref.dtype)


def _retention(q, k, v, g, s0, *, n_seq, c_real, precise=False):
    n_chunks, cp, qw = q.shape
    vw = v.shape[2]
    nc = n_chunks // n_seq
    dk, dv = qw // RET_HEADS, vw // RET_HEADS
    tok_map = lambda b, c: (b * nc + c, 0, 0)
    state_spec = pl.BlockSpec((1, RET_HEADS, dk, dv), lambda b, c: (b, 0, 0, 0))
    in_specs = [pl.BlockSpec((1, cp, qw), tok_map), pl.BlockSpec((1, cp, qw), tok_map),
                pl.BlockSpec((1, cp, vw), tok_map), pl.BlockSpec((1, cp, vw), tok_map)]
    args = [q, k, v, g]
    if s0 is not None:
        in_specs.append(state_spec)
        args.append(s0)
    return pl.pallas_call(
        functools.partial(_retention_kernel, heads=RET_HEADS, c_real=c_real, has_init=s0 is not None,
                          precise=precise),
        grid=(n_seq, nc),
        in_specs=in_specs,
        out_specs=[pl.BlockSpec((1, cp, vw), tok_map), state_spec],
        out_shape=[jax.ShapeDtypeStruct((n_chunks, cp, vw), F32 if precise else BF16),
                   jax.ShapeDtypeStruct((n_seq, RET_HEADS, dk, dv), F32)],
        compiler_params=_cparams("parallel", "arbitrary"), name="retention",
    )(*args)


def _page_block_means(page_refs, o_ref, ppb):
    for gi in range(len(page_refs) // ppb):
        acc = jnp.sum(page_refs[gi * ppb][0], axis=0)
        for p in range(1, ppb):
            acc = acc + jnp.sum(page_refs[gi * ppb + p][0], axis=0)
        o_ref[0, gi] = acc * (1.0 / (ppb * page_refs[0].shape[1]))


def _cache_means_kernel(pt_ref, *refs, ppb):
    del pt_ref
    _page_block_means(refs[:-1], refs[-1], ppb)


def _cache_means(cache_k, page_table, n_full, ppb):
    n_seq = page_table.shape[0]
    _, ps, heads, dh = cache_k.shape
    grp = 4 if n_full % 4 == 0 else 1

    def page_map(b, n, pt, *, gi, p):
        return (pt[b, (n * grp + gi) * ppb + p], 0, 0, 0)

    in_specs = [pl.BlockSpec((1, ps, heads, dh), functools.partial(page_map, gi=gi, p=p))
                for gi in range(grp) for p in range(ppb)]
    return pl.pallas_call(
        functools.partial(_cache_means_kernel, ppb=ppb),
        grid_spec=pltpu.PrefetchScalarGridSpec(
            num_scalar_prefetch=1, grid=(n_seq, n_full // grp), in_specs=in_specs,
            out_specs=pl.BlockSpec((1, grp, heads, dh), lambda b, n, pt: (b, n, 0, 0))),
        out_shape=jax.ShapeDtypeStruct((n_seq, n_full, heads, dh), F32),
        compiler_params=_cparams("parallel", "arbitrary"), name="cache_means",
    )(page_table, *([cache_k] * (grp * ppb)))


def _block_means_kernel(k_ref, o_ref, *, blk):
    for i in range(o_ref.shape[0]):
        o_ref[i:i + 1, :] = jnp.mean(k_ref[i * blk:(i + 1) * blk, :], axis=0, keepdims=True)


def _block_means(k2d, blk):
    m, w = k2d.shape
    n_blocks = m // blk
    per_step = 8 if n_blocks % 8 == 0 else n_blocks
    return pl.pallas_call(
        functools.partial(_block_means_kernel, blk=blk),
        grid=(n_blocks // per_step,),
        in_specs=[pl.BlockSpec((per_step * blk, w), lambda i: (i, 0))],
        out_specs=pl.BlockSpec((per_step, w), lambda i: (i, 0)),
        out_shape=jax.ShapeDtypeStruct((n_blocks, w), F32),
        compiler_params=_cparams("parallel"), name="block_means",
    )(k2d)


def _rank_rows(g):
    n = g.shape[0]
    sub = 8
    sub_i = lax.broadcasted_iota(jnp.int32, (sub, g.shape[1]), 0)
    groups = []
    for r0 in range(0, n, sub):
        gr = g[r0:r0 + sub, :]
        cnt = jnp.zeros(gr.shape, jnp.int32)
        for jp in range(n):
            row = g[jp:jp + 1, :]
            if jp < r0:
                inc = jnp.where(row >= gr, 1, 0)
            elif jp >= r0 + gr.shape[0] - 1:
                inc = jnp.where(row > gr, 1, 0)
            else:
                inc = jnp.where(sub_i[:gr.shape[0]] > jp - r0,
                                jnp.where(row >= gr, 1, 0), jnp.where(row > gr, 1, 0))
            cnt = cnt + inc
        groups.append(cnt)
    return jnp.concatenate(groups, axis=0)


def _moba_prompt_kernel(*refs, blk, topk, scale, grp, hps, n_page_refs, ppb):
    if n_page_refs:
        refs = refs[1:]
    q_ref, k_ref, v_ref, mean_ref, slope_ref = refs[:5]
    page_refs = refs[5:5 + n_page_refs]
    refs = refs[5 + n_page_refs:]
    if n_page_refs:
        o_ref, cmean_ref = refs[:2]
        refs = refs[2:]
    else:
        o_ref = refs[0]
        refs = refs[1:]
    vt_ref, sel_ref, u_ref, mx_ref = refs

    i = pl.program_id(2)
    nb = k_ref.shape[0] // blk
    dh = q_ref.shape[1] // hps
    inv_scale = 1.0 / scale
    exp2_coeff = scale * math.log2(math.e)
    hcols = lambda hh: slice(hh * dh, (hh + 1) * dh)

    @pl.when(i == 0)
    def _():
        for hh in range(hps):
            for jb in range(nb):
                vt_ref[hh, jb] = v_ref[jb * blk:(jb + 1) * blk, hcols(hh)].astype(F32).T.astype(BF16)

    if n_page_refs:
        _page_block_means(page_refs, cmean_ref, ppb)

    key_i = lax.broadcasted_iota(jnp.int32, (blk, blk), 0)
    qry_i = lax.broadcasted_iota(jnp.int32, (blk, blk), 1)
    dq = (qry_i - key_i).astype(F32)
    bidx = lax.broadcasted_iota(jnp.int32, (nb, blk), 0)
    past = bidx < i
    blk_dist = ((i - bidx) * blk).astype(F32)

    qts, dbiases, state = [], [], []
    for hh in range(hps):
        slope_u = slope_ref[hh][:, :1] * inv_scale
        qt = q_ref[:, hcols(hh)].astype(F32).T.astype(BF16)
        gate = jnp.dot(mean_ref[:, hcols(hh)].astype(BF16), qt, preferred_element_type=F32)
        gm = jnp.where(past, gate, -jnp.inf)
        cnt = _rank_rows(gm)
        sel_ref[hh] = jnp.where(past & (cnt < topk), 0.0, MASKED) - blk_dist * slope_u
        dbias = dq * slope_u
        k_own = k_ref[pl.ds(pl.multiple_of(i * blk, blk), blk), hcols(hh)]
        u = jnp.dot(k_own, qt, preferred_element_type=F32) - dbias
        u = jnp.where(dq >= 0, u, MASKED)
        m0 = jnp.max(u, axis=0, keepdims=True)
        p = jnp.exp2((u - m0) * exp2_coeff)
        l0 = jnp.sum(p, axis=0, keepdims=True)
        acc0 = jnp.dot(vt_ref[hh, i], p.astype(BF16), preferred_element_type=F32)
        qts.append(qt)
        dbiases.append(dbias)
        state.append((m0, l0, acc0))

    def stage_a(hh, j0):
        mx = None
        for g in range(grp):
            kj = k_ref[pl.ds(pl.multiple_of((j0 + g) * blk, blk), blk), hcols(hh)]
            u = jnp.dot(kj, qts[hh], preferred_element_type=F32) - dbiases[hh] + sel_ref[hh, pl.ds(j0 + g, 1), :]
            u_ref[hh, g * blk:(g + 1) * blk, :] = u
            mx = u if mx is None else jnp.maximum(mx, u)
        mx_ref[hh] = jnp.max(mx, axis=0, keepdims=True)

    def stage_b(hh, j0, m_old, l_old, acc_old):
        m_new = jnp.maximum(m_old, mx_ref[hh])
        alpha = jnp.exp2((m_old - m_new) * exp2_coeff)
        acc = alpha * acc_old
        l_el = None
        for g in range(grp):
            pg = jnp.exp2((u_ref[hh, g * blk:(g + 1) * blk, :] - m_new) * exp2_coeff)
            l_el = pg if l_el is None else l_el + pg
            acc = acc + jnp.dot(vt_ref[hh, j0 + g], pg.astype(BF16), preferred_element_type=F32)
        return m_new, alpha * l_old + jnp.sum(l_el, axis=0, keepdims=True), acc

    n_grp = jnp.maximum((i + grp - 1) // grp, 1)
    for hh in range(hps):
        stage_a(hh, 0)

    def trip(gi, carry):
        out = tuple(stage_b(hh, gi * grp, *carry[hh]) for hh in range(hps))
        for hh in range(hps):
            stage_a(hh, (gi + 1) * grp)
        return out

    state = lax.fori_loop(0, n_grp - 1, trip, tuple(state))
    for hh in range(hps):
        _, l_f, acc_f = stage_b(hh, (n_grp - 1) * grp, *state[hh])
        o_ref[:, hcols(hh)] = (acc_f / l_f).T.astype(o_ref.dtype)


def _moba_prompt(q, k, v, means, slopes, *, n_seq, cache_k=None, page_table=None):
    m, w = q.shape
    seq = m // n_seq
    dh = w // ATT_HEADS
    blk = MOBA_BLOCK
    nb = seq // blk
    hps = MOBA_HEADS_PER_STEP if ATT_HEADS % MOBA_HEADS_PER_STEP == 0 else 1
    grp = MOBA_KV_GROUP if nb % MOBA_KV_GROUP == 0 else 1
    n_hp = ATT_HEADS // hps
    n_steps = n_seq * n_hp * nb

    bps = 0
    if cache_k is not None:
        n_dec, n_pages = page_table.shape
        _, ps, c_heads, c_dh = cache_k.shape
        ppb = blk // ps
        n_full = n_pages // ppb
        if (n_dec * n_full) % n_steps == 0 and n_full % ((n_dec * n_full) // n_steps) == 0:
            bps = (n_dec * n_full) // n_steps
    fused = bps > 0

    def tok_map(b, h, i, *_):
        return (b * nb + i, h)

    def seq_map(b, h, i, *_):
        return (b, h)

    in_specs = [pl.BlockSpec((blk, hps * dh), tok_map),
                pl.BlockSpec((seq, hps * dh), seq_map),
                pl.BlockSpec((seq, hps * dh), seq_map),
                pl.BlockSpec((nb, hps * dh), seq_map),
                pl.BlockSpec((hps, 1, LANES), lambda b, h, i, *_: (h, 0, 0))]
    out_specs = [pl.BlockSpec((blk, hps * dh), tok_map)]
    out_shape = [jax.ShapeDtypeStruct((m, w), BF16)]
    args = [q, k, v, means, slopes]
    prefetch = []
    if fused:
        steps_per_seq = n_full // bps

        def step_of(b, h, i):
            return (b * n_hp + h) * nb + i

        def page_map(b, h, i, pt, *, n):
            s = step_of(b, h, i)
            return (pt[s // steps_per_seq, (s % steps_per_seq) * bps * ppb + n], 0, 0, 0)

        in_specs += [pl.BlockSpec((1, ps, c_heads, c_dh), functools.partial(page_map, n=n))
                     for n in range(bps * ppb)]
        out_specs.append(pl.BlockSpec(
            (1, bps, c_heads, c_dh),
            lambda b, h, i, pt: (step_of(b, h, i) // steps_per_seq, step_of(b, h, i) % steps_per_seq, 0, 0)))
        out_shape.append(jax.ShapeDtypeStruct((n_dec, n_full, c_heads, c_dh), F32))
        args += [cache_k] * (bps * ppb)
        prefetch = [page_table]

    outs = pl.pallas_call(
        functools.partial(_moba_prompt_kernel, blk=blk, topk=MOBA_TOPK, scale=dh ** -0.5, grp=grp, hps=hps,
                          n_page_refs=bps * ppb if fused else 0, ppb=ppb if fused else 1),
        grid_spec=pltpu.PrefetchScalarGridSpec(
            num_scalar_prefetch=len(prefetch), grid=(n_seq, n_hp, nb),
            in_specs=in_specs, out_specs=out_specs,
            scratch_shapes=[pltpu.VMEM((hps, nb, dh, blk), BF16),
                            pltpu.VMEM((hps, nb, blk), F32),
                            pltpu.VMEM((hps, grp * blk, blk), F32),
                            pltpu.VMEM((hps, 1, blk), F32)]),
        out_shape=out_shape,
        compiler_params=_cparams("arbitrary", "arbitrary", "arbitrary"), name="moba_prompt",
    )(*prefetch, *args)
    return outs[0], (outs[1] if fused else None)


def _sample_topk_kernel(q_ref, mean_ref, o_ref, *, heads, topk):
    t_n = q_ref.shape[1]
    dh = q_ref.shape[2] // heads
    nb = mean_ref.shape[1]
    lane = lax.broadcasted_iota(jnp.int32, (t_n, LANES), 1)
    bidx = lax.broadcasted_iota(jnp.int32, (t_n, nb), 1)
    out = jnp.zeros((t_n, LANES), jnp.int32)
    for h in range(heads):
        qh = q_ref[0, :, h * dh:(h + 1) * dh]
        mh = mean_ref[0, :, h * dh:(h + 1) * dh]
        gate = _mm(qh, mh, NT_DIMS, precise=True)
        cnt = jnp.zeros((t_n, nb), jnp.int32)
        for jp in range(nb):
            colv = gate[:, jp:jp + 1]
            beats = (colv > gate) | ((colv == gate) & (jp < bidx))
            cnt = cnt + beats.astype(jnp.int32)
        for r in range(topk):
            idx_r = jnp.sum(jnp.where(cnt == r, bidx, 0), axis=1, keepdims=True)
            out = jnp.where(lane == h * topk + r, idx_r, out)
    o_ref[0] = out


def _sample_topk(q3, means3):
    n_seq, t_n, w = q3.shape
    nb = means3.shape[1]
    return pl.pallas_call(
        functools.partial(_sample_topk_kernel, heads=ATT_HEADS, topk=MOBA_TOPK),
        grid=(n_seq,),
        in_specs=[pl.BlockSpec((1, t_n, w), lambda b: (b, 0, 0)),
                  pl.BlockSpec((1, nb, w), lambda b: (b, 0, 0))],
        out_specs=pl.BlockSpec((1, t_n, LANES), lambda b: (b, 0, 0)),
        out_shape=jax.ShapeDtypeStruct((n_seq, t_n, LANES), jnp.int32),
        compiler_params=_cparams("parallel"), name="sample_topk",
    )(q3, means3)


def _moba_sample_kernel(idx_ref, pt_ref, q_ref, kn_ref, vn_ref, slope_ref, *refs,
                        heads, topk, ppb, past_len, blk, scale):
    del pt_ref
    t_n = q_ref.shape[1]
    n_slots = t_n * topk * ppb
    k_refs, v_refs, o_ref = refs[:n_slots], refs[n_slots:2 * n_slots], refs[2 * n_slots]
    ps, dh = k_refs[0].shape[0], k_refs[0].shape[2]
    b = pl.program_id(0)
    h = pl.program_id(1)
    q = q_ref[0]
    slope = slope_ref[0][:, :1]
    row_i = lax.broadcasted_iota(jnp.int32, (t_n, ps), 0)
    lane_i = lax.broadcasted_iota(jnp.int32, (t_n, ps), 1)

    scores = []
    for t in range(t_n):
        for r in range(topk):
            blk_idx = idx_ref[((b * t_n + t) * heads + h) * topk + r]
            for p in range(ppb):
                kp = k_refs[(t * topk + r) * ppb + p].reshape(ps, dh)[...]
                s = lax.dot_general(q, kp, NT_DIMS, preferred_element_type=F32) * scale
                dist0 = past_len + t - blk_idx * blk - p * ps
                s = s - slope * (dist0 - lane_i).astype(F32)
                scores.append(jnp.where(row_i == t, s, MASKED))

    ro = lax.broadcasted_iota(jnp.int32, (t_n, t_n), 0)
    co = lax.broadcasted_iota(jnp.int32, (t_n, t_n), 1)
    s_own = lax.dot_general(q, kn_ref[0], NT_DIMS, preferred_element_type=F32) * scale
    s_own = s_own - slope * (ro - co).astype(F32)
    s_own = jnp.where(co <= ro, s_own, MASKED)

    m_el = scores[0]
    for s in scores[1:]:
        m_el = jnp.maximum(m_el, s)
    m = jnp.maximum(jnp.max(m_el, axis=1, keepdims=True), jnp.max(s_own, axis=1, keepdims=True))
    p_own = jnp.exp(s_own - m)
    acc = jnp.dot(p_own, vn_ref[0], preferred_element_type=F32)
    l_el = jnp.zeros((t_n, ps), F32)
    for slot, s in enumerate(scores):
        pe = jnp.exp(s - m)
        l_el = l_el + pe
        acc = acc + jnp.dot(pe, v_refs[slot].reshape(ps, dh)[...], preferred_element_type=F32)
    l = jnp.sum(l_el, axis=1, keepdims=True) + jnp.sum(p_own, axis=1, keepdims=True)
    o_ref[0] = acc / l


def _moba_sample(q3, kn3, vn3, cache_k, cache_v, page_table, idx_flat, slopes, *, past_len):
    n_seq, t_n, w = q3.shape
    n_pool, ps, heads, dh = cache_k.shape
    ppb = MOBA_BLOCK // ps
    ck = cache_k.reshape(n_pool, ps, heads, 1, dh)
    cv = cache_v.reshape(n_pool, ps, heads, 1, dh)

    def page_map(b, h, idx, pt, *, t, r, p):
        blk_idx = idx[((b * t_n + t) * heads + h) * MOBA_TOPK + r]
        return (pt[b, blk_idx * ppb + p], 0, h, 0, 0)

    slot_specs = [pl.BlockSpec((None, ps, None, 1, dh), functools.partial(page_map, t=t, r=r, p=p))
                  for t in range(t_n) for r in range(MOBA_TOPK) for p in range(ppb)]
    tok_spec = pl.BlockSpec((1, t_n, dh), lambda b, h, idx, pt: (b, 0, h))
    n_slots = len(slot_specs)
    return pl.pallas_call(
        functools.partial(_moba_sample_kernel, heads=heads, topk=MOBA_TOPK, ppb=ppb, past_len=past_len,
                          blk=MOBA_BLOCK, scale=dh ** -0.5),
        grid_spec=pltpu.PrefetchScalarGridSpec(
            num_scalar_prefetch=2, grid=(n_seq, heads),
            in_specs=[tok_spec, tok_spec, tok_spec,
                      pl.BlockSpec((1, 1, LANES), lambda b, h, idx, pt: (h, 0, 0))] + slot_specs + slot_specs,
            out_specs=tok_spec),
        out_shape=jax.ShapeDtypeStruct((n_seq, t_n, w), F32),
        compiler_params=_cparams("parallel", "arbitrary"), name="moba_sample",
    )(idx_flat, page_table, q3, kn3, vn3, slopes, *([ck] * n_slots), *([cv] * n_slots))


def _alibi_slope_rows():
    s = jnp.float32(2.0) ** (-8.0 * (jnp.arange(ATT_HEADS, dtype=F32) + 1.0) / ATT_HEADS)
    return jnp.broadcast_to(s[:, None, None], (ATT_HEADS, 1, LANES))


def _decoder(x, retention_fn, attention_fn, w, depth, is_prompt):
    n_seq, seq, dm = x.shape
    m = n_seq * seq
    n_ret = depth // 2
    alpha = (2 * depth) ** 0.25
    hq = w["ret_in"][0].shape[1] // 6
    dk = hq // RET_HEADS
    aw = w["kv"].shape[1] // 2
    h = x.reshape(m, dm)
    states = []
    k_f32 = v_f32 = k_b = v_b = None
    precise = not is_prompt
    act = F32 if precise else BF16
    for layer in range(depth):
        if layer < n_ret:
            segs = ((0, hq, 1.0, (act,)), (hq, hq, dk ** -0.5, (act,)),
                    (2 * hq, 2 * hq, 1.0, (act,)), (4 * hq, 2 * hq, 1.0, (act,)))
            q, k, v, g = _proj(h, w["ret_in"][layer], segs, precise)
            mix_in, s_new = retention_fn(layer, q, k, v, g)
            states.append(s_new)
            w_mix = w["ret_out"][layer]
        else:
            j = layer - n_ret
            if j == 0:
                k_dt = (F32, BF16) if is_prompt else (F32,)
                q_dt = (BF16,) if is_prompt else (F32,)
                segs = ((0, aw, 1.0, k_dt), (aw, aw, 1.0, k_dt), (2 * aw, aw, 1.0, q_dt))
                outs = _proj(h, jnp.concatenate([w["kv"], w["q"][j]], axis=1), segs, precise)
                if is_prompt:
                    k_f32, k_b, v_f32, v_b, qa = outs
                else:
                    k_f32, v_f32, qa = outs
            else:
                (qa,) = _proj(h, w["q"][j], ((0, aw, 1.0, (BF16,) if is_prompt else (F32,)),), precise)
            mix_in = attention_fn(j, qa, k_f32, v_f32, k_b, v_b)
            w_mix = w["att_out"][j]
        h = _outproj_ln(mix_in, w_mix, h, w["ln_g"][layer, 0], w["ln_b"][layer, 0], alpha, precise)
        f = _ffn_in(h, w["ffn_in"][layer], precise)
        h = _outproj_ln(f, w["ffn_out"][layer], h, w["ln_g"][layer, 1], w["ln_b"][layer, 1], alpha, precise)
    return h.reshape(n_seq, seq, dm), jnp.stack(states), k_f32, v_f32


def kernel(x_prompt, x_sample, cache_k, cache_v, state_ret, page_table, w_ret_in, w_ret_out, w_kv,
           w_q, w_att_out, w_ffn_in, w_ffn_out, ln_g, ln_b):
    depth = w_ffn_in.shape[0]
    n_b, seq, _ = x_prompt.shape
    n_d, t_n, _ = x_sample.shape
    ps = cache_k.shape[1]
    n_pages = page_table.shape[1]
    past_len = n_pages * ps
    ppb = MOBA_BLOCK // ps
    n_full = past_len // MOBA_BLOCK
    assert seq % MOBA_BLOCK == 0 and seq % RET_CHUNK == 0
    assert n_full * ppb == n_pages and n_full >= MOBA_TOPK
    assert t_n <= SAMPLE_CHUNK_PAD

    w32 = {"ret_in": w_ret_in, "ret_out": w_ret_out, "kv": w_kv, "q": w_q, "att_out": w_att_out,
           "ffn_in": w_ffn_in, "ffn_out": w_ffn_out, "ln_g": ln_g, "ln_b": ln_b}
    w = {name: (a if name.startswith("ln_") else a.astype(BF16)) for name, a in w32.items()}
    slopes = _alibi_slope_rows()
    cache_means = [None]

    def ret_prompt(layer, q, k, v, g):
        del layer
        c3 = lambda a: a.reshape(-1, RET_CHUNK, a.shape[-1])
        o, s = _retention(c3(q), c3(k), c3(v), c3(g), None, n_seq=n_b, c_real=RET_CHUNK)
        return o.reshape(n_b * seq, -1), s

    def att_prompt(j, q, k_f32, v_f32, k_b, v_b):
        del v_f32
        means = _block_means(k_f32, MOBA_BLOCK)
        if j == 0:
            o, cache_means[0] = _moba_prompt(q, k_b, v_b, means, slopes, n_seq=n_b,
                                             cache_k=cache_k, page_table=page_table)
        else:
            o, _ = _moba_prompt(q, k_b, v_b, means, slopes, n_seq=n_b)
        return o

    y_p, s_p, k_p, v_p = _decoder(x_prompt, ret_prompt, att_prompt, w, depth, True)

    def ret_sample(layer, q, k, v, g):
        def c3(a):
            a = a.reshape(n_d, t_n, a.shape[-1])
            return jnp.pad(a, ((0, 0), (0, SAMPLE_CHUNK_PAD - t_n), (0, 0)))
        o, s = _retention(c3(q), c3(k), c3(v), c3(g), state_ret[layer].astype(F32), n_seq=n_d, c_real=t_n,
                          precise=True)
        return o[:, :t_n].reshape(n_d * t_n, -1), s

    def att_sample(j, q, k_f32, v_f32, k_b, v_b):
        del j, k_b, v_b
        r3 = lambda a: a.reshape(n_d, t_n, a.shape[-1])
        if cache_means[0] is None:
            cache_means[0] = _cache_means(cache_k, page_table, n_full, ppb)
        idx = _sample_topk(r3(q), cache_means[0].reshape(n_d, n_full, -1))
        idx_flat = idx[:, :, :ATT_HEADS * MOBA_TOPK].reshape(-1)
        o = _moba_sample(r3(q), r3(k_f32), r3(v_f32), cache_k, cache_v, page_table, idx_flat, slopes,
                         past_len=past_len)
        return o.reshape(n_d * t_n, -1)

    y_s, s_s, k_s, v_s = _decoder(x_sample, ret_sample, att_sample, w32, depth, False)

    dh = k_p.shape[-1] // ATT_HEADS
    kv4 = lambda a, n, l: a.reshape(n, l, ATT_HEADS, dh)
    return (y_p, y_s, s_p, s_s, kv4(k_p, n_b, seq), kv4(v_p, n_b, seq), kv4(k_s, n_d, t_n), kv4(v_s, n_d, t_n))
```

```python
import functools
import math

import jax
import jax.numpy as jnp
from jax import lax
from jax.experimental import pallas as pl
from jax.experimental.pallas import tpu as pltpu

F32 = jnp.float32
BF16 = jnp.bfloat16

RET_HEADS = 4
RET_CHUNK = 128
ATT_HEADS = 8
MOBA_BLOCK = 256
MOBA_TOPK = 3
PAGE_SIZE = 128
LN_EPS = 1e-5
GN_EPS = 1e-6
MASKED = -1e30
V7X_VMEM_LIMIT_BYTES = 56 * 1024 * 1024
SAMPLE_CHUNK_PAD = 16
LANES = 128
MOBA_KV_GROUP = 4
MOBA_HEADS_PER_STEP = 2

NT_DIMS = (((1,), (1,)), ((), ()))
TN_DIMS = (((0,), (0,)), ((), ()))


def _cparams(*sem):
    return pltpu.CompilerParams(dimension_semantics=sem, vmem_limit_bytes=V7X_VMEM_LIMIT_BYTES)


def _row_tile(m):
    for t in (512, 256, 128, 64, 32, 16, 8):
        if m % t == 0:
            return t
    return m


def _mm(a, b, dims=None, *, precise=False):
    dims = dims or (((a.ndim - 1,), (0,)), ((), ()))
    if precise:
        return lax.dot_general(a.astype(F32), b.astype(F32), dims, precision=lax.Precision.HIGHEST,
                               preferred_element_type=F32)
    return lax.dot_general(a.astype(BF16), b.astype(BF16), dims, preferred_element_type=F32)


def _proj_kernel(x_ref, w_ref, *o_refs, segs, chunk, precise, means_blk):
    xb = x_ref[...] if precise else x_ref[...].astype(BF16)
    oi = 0
    for si, (start, width, scale, dtypes) in enumerate(segs):
        for c0 in range(0, width, chunk):
            cw = min(chunk, width - c0)
            acc = _mm(xb, w_ref[:, start + c0:start + c0 + cw], precise=precise)
            if scale != 1.0:
                acc = acc * scale
            for d, dt in enumerate(dtypes):
                o_refs[oi + d][:, c0:c0 + cw] = acc.astype(dt)
            if means_blk and si == 0:
                n_blk = acc.shape[0] // means_blk
                o_refs[-1][:, 0, c0:c0 + cw] = jnp.mean(acc.reshape(n_blk, means_blk, cw), axis=1)
        oi += len(dtypes)


def _proj(x, w, segs, precise=False, means_blk=0):
    m, kdim = x.shape
    tm = _row_tile(m)
    if precise:
        outs = []
        for start, width, scale, dtypes in segs:
            assert start % width == 0
            outs += pl.pallas_call(
                functools.partial(_proj_kernel, segs=((0, width, scale, dtypes),), chunk=512, precise=True,
                                  means_blk=0),
                grid=(m // tm,),
                in_specs=[pl.BlockSpec((tm, kdim), lambda i: (i, 0)),
                          pl.BlockSpec((kdim, width), functools.partial(lambda i, c: (0, c), c=start // width))],
                out_specs=[pl.BlockSpec((tm, width), lambda i: (i, 0)) for _ in dtypes],
                out_shape=[jax.ShapeDtypeStruct((m, width), dt) for dt in dtypes],
                compiler_params=_cparams("parallel"), name="proj_precise",
            )(x, w)
        return outs
    out_shape, out_specs = [], []
    for _, width, _, dtypes in segs:
        for dt in dtypes:
            out_shape.append(jax.ShapeDtypeStruct((m, width), dt))
            out_specs.append(pl.BlockSpec((tm, width), lambda i: (i, 0)))
    if means_blk:
        assert tm % means_blk == 0
        out_shape.append(jax.ShapeDtypeStruct((m // means_blk, 1, segs[0][1]), F32))
        out_specs.append(pl.BlockSpec((tm // means_blk, 1, segs[0][1]), lambda i: (i, 0, 0)))
    return pl.pallas_call(
        functools.partial(_proj_kernel, segs=segs, chunk=512, precise=False, means_blk=means_blk),
        grid=(m // tm,),
        in_specs=[pl.BlockSpec((tm, kdim), lambda i: (i, 0)),
                  pl.BlockSpec(w.shape, lambda i: (0, 0))],
        out_specs=out_specs, out_shape=out_shape,
        compiler_params=_cparams("parallel"), name="proj",
    )(x, w)


def _outproj_ln_kernel(a_ref, w_ref, r_ref, g_ref, b_ref, o_ref, *, alpha, precise):
    acc = _mm(a_ref[...], w_ref[...], precise=precise)
    y = alpha * r_ref[...] + acc
    mu = jnp.mean(y, axis=-1, keepdims=True)
    d = y - mu
    var = jnp.mean(d * d, axis=-1, keepdims=True)
    o_ref[...] = d * lax.rsqrt(var + LN_EPS) * g_ref[...] + b_ref[...]


def _outproj_ln(a, w, layer, res, g, b, alpha, precise=False):
    m, kdim = a.shape
    dm = w.shape[2]
    tm = _row_tile(m)
    return pl.pallas_call(
        functools.partial(_outproj_ln_kernel, alpha=alpha, precise=precise),
        grid=(m // tm,),
        in_specs=[pl.BlockSpec((tm, kdim), lambda i: (i, 0)),
                  pl.BlockSpec((None, kdim, dm), lambda i: (layer, 0, 0)),
                  pl.BlockSpec((tm, dm), lambda i: (i, 0)),
                  pl.BlockSpec((1, dm), lambda i: (0, 0)),
                  pl.BlockSpec((1, dm), lambda i: (0, 0))],
        out_specs=pl.BlockSpec((tm, dm), lambda i: (i, 0)),
        out_shape=jax.ShapeDtypeStruct((m, dm), F32),
        compiler_params=_cparams("parallel"), name="outproj_ln",
    )(a, w, res, g.reshape(1, dm), b.reshape(1, dm))


def _ffn_in_kernel(h_ref, w_ref, o_ref, *, dff, chunk):
    hb = h_ref[...].astype(BF16)
    for c0 in range(0, dff, chunk):
        cw = min(chunk, dff - c0)
        gate = jnp.dot(hb, w_ref[:, c0:c0 + cw], preferred_element_type=F32)
        up = jnp.dot(hb, w_ref[:, dff + c0:dff + c0 + cw], preferred_element_type=F32)
        o_ref[:, c0:c0 + cw] = (gate * jax.nn.sigmoid(gate) * up).astype(o_ref.dtype)


def _ffn_in_chunk_kernel(h_ref, wg_ref, wu_ref, o_ref):
    gate = _mm(h_ref[...], wg_ref[...], precise=True)
    up = _mm(h_ref[...], wu_ref[...], precise=True)
    o_ref[...] = gate * jax.nn.sigmoid(gate) * up


def _ffn_in(h, w, layer, precise=False):
    m, dm = h.shape
    dff = w.shape[2] // 2
    tm = _row_tile(m)
    if precise:
        tn = 256
        assert dff % tn == 0 and m == tm
        n_chunks = dff // tn
        return pl.pallas_call(
            _ffn_in_chunk_kernel,
            grid=(n_chunks,),
            in_specs=[pl.BlockSpec((m, dm), lambda c: (0, 0)),
                      pl.BlockSpec((None, dm, tn), lambda c: (layer, 0, c)),
                      pl.BlockSpec((None, dm, tn), lambda c: (layer, 0, n_chunks + c))],
            out_specs=pl.BlockSpec((m, tn), lambda c: (0, c)),
            out_shape=jax.ShapeDtypeStruct((m, dff), F32),
            compiler_params=_cparams("parallel"), name="ffn_in_precise",
        )(h, w, w)
    return pl.pallas_call(
        functools.partial(_ffn_in_kernel, dff=dff, chunk=512),
        grid=(m // tm,),
        in_specs=[pl.BlockSpec((tm, dm), lambda i: (i, 0)),
                  pl.BlockSpec((None,) + w.shape[1:], lambda i: (layer, 0, 0))],
        out_specs=pl.BlockSpec((tm, dff), lambda i: (i, 0)),
        out_shape=jax.ShapeDtypeStruct((m, dff), BF16),
        compiler_params=_cparams("parallel"), name="ffn_in",
    )(h, w)


def _retention_kernel(*refs, heads, c_real, has_init, precise):
    if has_init:
        q_ref, k_ref, v_ref, g_ref, s0_ref, o_ref, s_ref = refs
    else:
        q_ref, k_ref, v_ref, g_ref, o_ref, s_ref = refs

    @pl.when(pl.program_id(1) == 0)
    def _():
        if has_init:
            s_ref[...] = s0_ref[...]
        else:
            s_ref[...] = jnp.zeros(s_ref.shape, F32)

    cp = q_ref.shape[1]
    dk = q_ref.shape[2] // heads
    dv = v_ref.shape[2] // heads
    row = lax.broadcasted_iota(jnp.int32, (cp, cp), 0)
    col = lax.broadcasted_iota(jnp.int32, (cp, cp), 1)
    diff = (row - col).astype(F32)
    ri = lax.broadcasted_iota(jnp.int32, (cp, 1), 0).astype(F32)
    for h in range(heads):
        lg = math.log(1.0 - 2.0 ** (-5.0 - h))
        decay = jnp.where(diff >= 0, jnp.exp(jnp.maximum(diff, 0.0) * lg), 0.0)
        xi = jnp.exp((ri + 1.0) * lg)
        zeta = jnp.exp((c_real - 1.0 - ri) * lg)
        qh = q_ref[0, :, h * dk:(h + 1) * dk]
        kh = k_ref[0, :, h * dk:(h + 1) * dk]
        vh = v_ref[0, :, h * dv:(h + 1) * dv]
        s_old = s_ref[0, h]
        scores = _mm(qh, kh, NT_DIMS, precise=precise) * decay
        intra = _mm(scores, vh, precise=precise)
        cross = _mm(qh, s_old, precise=precise) * xi
        upd = _mm(kh.astype(F32) * zeta, vh, TN_DIMS, precise=precise)
        s_ref[0, h] = math.exp(c_real * lg) * s_old + upd
        o = intra + cross
        mu = jnp.mean(o, axis=-1, keepdims=True)
        d = o - mu
        var = jnp.mean(d * d, axis=-1, keepdims=True)
        on = d * lax.rsqrt(var + GN_EPS)
        gh = g_ref[0, :, h * dv:(h + 1) * dv].astype(F32)
        o_ref[0, :, h * dv:(h + 1) * dv] = (gh * jax.nn.sigmoid(gh) * on).astype(o_ref.dtype)


def _retention(q, k, v, g, s0, *, n_seq, c_real, precise=False):
    n_chunks, cp, qw = q.shape
    vw = v.shape[2]
    nc = n_chunks // n_seq
    dk, dv = qw // RET_HEADS, vw // RET_HEADS
    tok_map = lambda b, c: (b * nc + c, 0, 0)
    state_spec = pl.BlockSpec((1, RET_HEADS, dk, dv), lambda b, c: (b, 0, 0, 0))
    in_specs = [pl.BlockSpec((1, cp, qw), tok_map), pl.BlockSpec((1, cp, qw), tok_map),
                pl.BlockSpec((1, cp, vw), tok_map), pl.BlockSpec((1, cp, vw), tok_map)]
    args = [q, k, v, g]
    if s0 is not None:
        in_specs.append(state_spec)
        args.append(s0)
    return pl.pallas_call(
        functools.partial(_retention_kernel, heads=RET_HEADS, c_real=c_real, has_init=s0 is not None,
                          precise=precise),
        grid=(n_seq, nc),
        in_specs=in_specs,
        out_specs=[pl.BlockSpec((1, cp, vw), tok_map), state_spec],
        out_shape=[jax.ShapeDtypeStruct((n_chunks, cp, vw), F32 if precise else BF16),
                   jax.ShapeDtypeStruct((n_seq, RET_HEADS, dk, dv), F32)],
        compiler_params=_cparams("parallel", "arbitrary"), name="retention",
    )(*args)


def _page_block_means(page_refs, o_ref, ppb):
    for gi in range(len(page_refs) // ppb):
        acc = jnp.sum(page_refs[gi * ppb][0], axis=0)
        for p in range(1, ppb):
            acc = acc + jnp.sum(page_refs[gi * ppb + p][0], axis=0)
        o_ref[0, gi] = acc * (1.0 / (ppb * page_refs[0].shape[1]))


def _cache_means_kernel(pt_ref, *refs, ppb):
    del pt_ref
    _page_block_means(refs[:-1], refs[-1], ppb)


def _cache_means(cache_k, page_table, n_full, ppb):
    n_seq = page_table.shape[0]
    _, ps, heads, dh = cache_k.shape
    grp = 4 if n_full % 4 == 0 else 1

    def page_map(b, n, pt, *, gi, p):
        return (pt[b, (n * grp + gi) * ppb + p], 0, 0, 0)

    in_specs = [pl.BlockSpec((1, ps, heads, dh), functools.partial(page_map, gi=gi, p=p))
                for gi in range(grp) for p in range(ppb)]
    return pl.pallas_call(
        functools.partial(_cache_means_kernel, ppb=ppb),
        grid_spec=pltpu.PrefetchScalarGridSpec(
            num_scalar_prefetch=1, grid=(n_seq, n_full // grp), in_specs=in_specs,
            out_specs=pl.BlockSpec((1, grp, heads, dh), lambda b, n, pt: (b, n, 0, 0))),
        out_shape=jax.ShapeDtypeStruct((n_seq, n_full, heads, dh), F32),
        compiler_params=_cparams("parallel", "arbitrary"), name="cache_means",
    )(page_table, *([cache_k] * (grp * ppb)))


def _rank_rows(g):
    n = g.shape[0]
    sub = 8
    sub_i = lax.broadcasted_iota(jnp.int32, (sub, g.shape[1]), 0)
    groups = []
    for r0 in range(0, n, sub):
        gr = g[r0:r0 + sub, :]
        cnt = jnp.zeros(gr.shape, jnp.int32)
        for jp in range(n):
            row = g[jp:jp + 1, :]
            if jp < r0:
                inc = jnp.where(row >= gr, 1, 0)
            elif jp >= r0 + gr.shape[0] - 1:
                inc = jnp.where(row > gr, 1, 0)
            else:
                inc = jnp.where(sub_i[:gr.shape[0]] > jp - r0,
                                jnp.where(row >= gr, 1, 0), jnp.where(row > gr, 1, 0))
            cnt = cnt + inc
        groups.append(cnt)
    return jnp.concatenate(groups, axis=0)


def _moba_prompt_kernel(*refs, blk, topk, scale, grp, hps, n_page_refs, ppb):
    if n_page_refs:
        refs = refs[1:]
    q_ref, k_ref, v_ref, mean_ref, slope_ref = refs[:5]
    page_refs = refs[5:5 + n_page_refs]
    refs = refs[5 + n_page_refs:]
    if n_page_refs:
        o_ref, cmean_ref = refs[:2]
        refs = refs[2:]
    else:
        o_ref = refs[0]
        refs = refs[1:]
    vt_ref, sel_ref, u_ref, mx_ref = refs

    i = pl.program_id(2)
    nb = k_ref.shape[0] // blk
    dh = q_ref.shape[1] // hps
    inv_scale = 1.0 / scale
    exp2_coeff = scale * math.log2(math.e)
    hcols = lambda hh: slice(hh * dh, (hh + 1) * dh)

    @pl.when(i == 0)
    def _():
        for hh in range(hps):
            for jb in range(nb):
                vt_ref[hh, jb] = v_ref[jb * blk:(jb + 1) * blk, hcols(hh)].astype(F32).T.astype(BF16)

    if n_page_refs:
        _page_block_means(page_refs, cmean_ref, ppb)

    key_i = lax.broadcasted_iota(jnp.int32, (blk, blk), 0)
    qry_i = lax.broadcasted_iota(jnp.int32, (blk, blk), 1)
    dq = (qry_i - key_i).astype(F32)
    bidx = lax.broadcasted_iota(jnp.int32, (nb, blk), 0)
    past = bidx < i
    blk_dist = ((i - bidx) * blk).astype(F32)

    qts, dbiases, state = [], [], []
    for hh in range(hps):
        slope_u = slope_ref[hh][:, :1] * inv_scale
        qt = q_ref[:, hcols(hh)].astype(F32).T.astype(BF16)
        gate = jnp.dot(mean_ref[:, hcols(hh)].astype(BF16), qt, preferred_element_type=F32)
        gm = jnp.where(past, gate, -jnp.inf)
        cnt = _rank_rows(gm)
        sel_ref[hh] = jnp.where(past & (cnt < topk), 0.0, MASKED) - blk_dist * slope_u
        dbias = dq * slope_u
        k_own = k_ref[pl.ds(pl.multiple_of(i * blk, blk), blk), hcols(hh)]
        u = jnp.dot(k_own, qt, preferred_element_type=F32) - dbias
        u = jnp.where(dq >= 0, u, MASKED)
        m0 = jnp.max(u, axis=0, keepdims=True)
        p = jnp.exp2((u - m0) * exp2_coeff)
        l0 = jnp.sum(p, axis=0, keepdims=True)
        acc0 = jnp.dot(vt_ref[hh, i], p.astype(BF16), preferred_element_type=F32)
        qts.append(qt)
        dbiases.append(dbias)
        state.append((m0, l0, acc0))

    def stage_a(hh, j0):
        mx = None
        for g in range(grp):
            kj = k_ref[pl.ds(pl.multiple_of((j0 + g) * blk, blk), blk), hcols(hh)]
            u = jnp.dot(kj, qts[hh], preferred_element_type=F32) - dbiases[hh] + sel_ref[hh, pl.ds(j0 + g, 1), :]
            u_ref[hh, g * blk:(g + 1) * blk, :] = u
            mx = u if mx is None else jnp.maximum(mx, u)
        mx_ref[hh] = jnp.max(mx, axis=0, keepdims=True)

    def stage_b(hh, j0, m_old, l_old, acc_old):
        m_new = jnp.maximum(m_old, mx_ref[hh])
        alpha = jnp.exp2((m_old - m_new) * exp2_coeff)
        acc = alpha * acc_old
        l_el = None
        for g in range(grp):
            pg = jnp.exp2((u_ref[hh, g * blk:(g + 1) * blk, :] - m_new) * exp2_coeff)
            l_el = pg if l_el is None else l_el + pg
            acc = acc + jnp.dot(vt_ref[hh, j0 + g], pg.astype(BF16), preferred_element_type=F32)
        return m_new, alpha * l_old + jnp.sum(l_el, axis=0, keepdims=True), acc

    n_grp = jnp.maximum((i + grp - 1) // grp, 1)
    for hh in range(hps):
        stage_a(hh, 0)

    def trip(gi, carry):
        out = tuple(stage_b(hh, gi * grp, *carry[hh]) for hh in range(hps))
        for hh in range(hps):
            stage_a(hh, (gi + 1) * grp)
        return out

    state = lax.fori_loop(0, n_grp - 1, trip, tuple(state))
    for hh in range(hps):
        _, l_f, acc_f = stage_b(hh, (n_grp - 1) * grp, *state[hh])
        o_ref[:, hcols(hh)] = (acc_f / l_f).T.astype(o_ref.dtype)


def _moba_prompt(q, k, v, means, slopes, *, n_seq, cache_k=None, page_table=None):
    m, w = q.shape
    seq = m // n_seq
    dh = w // ATT_HEADS
    blk = MOBA_BLOCK
    nb = seq // blk
    hps = MOBA_HEADS_PER_STEP if ATT_HEADS % MOBA_HEADS_PER_STEP == 0 else 1
    grp = MOBA_KV_GROUP if nb % MOBA_KV_GROUP == 0 else 1
    n_hp = ATT_HEADS // hps
    n_steps = n_seq * n_hp * nb

    bps = 0
    if cache_k is not None:
        n_dec, n_pages = page_table.shape
        _, ps, c_heads, c_dh = cache_k.shape
        ppb = blk // ps
        n_full = n_pages // ppb
        if (n_dec * n_full) % n_steps == 0 and n_full % ((n_dec * n_full) // n_steps) == 0:
            bps = (n_dec * n_full) // n_steps
    fused = bps > 0

    def tok_map(b, h, i, *_):
        return (b * nb + i, h)

    def seq_map(b, h, i, *_):
        return (b, h)

    in_specs = [pl.BlockSpec((blk, hps * dh), tok_map),
                pl.BlockSpec((seq, hps * dh), seq_map),
                pl.BlockSpec((seq, hps * dh), seq_map),
                pl.BlockSpec((nb, hps * dh), seq_map),
                pl.BlockSpec((hps, 1, LANES), lambda b, h, i, *_: (h, 0, 0))]
    out_specs = [pl.BlockSpec((blk, hps * dh), tok_map)]
    out_shape = [jax.ShapeDtypeStruct((m, w), BF16)]
    args = [q, k, v, means, slopes]
    prefetch = []
    if fused:
        steps_per_seq = n_full // bps

        def step_of(b, h, i):
            return (b * n_hp + h) * nb + i

        def page_map(b, h, i, pt, *, n):
            s = step_of(b, h, i)
            return (pt[s // steps_per_seq, (s % steps_per_seq) * bps * ppb + n], 0, 0, 0)

        in_specs += [pl.BlockSpec((1, ps, c_heads, c_dh), functools.partial(page_map, n=n))
                     for n in range(bps * ppb)]
        out_specs.append(pl.BlockSpec(
            (1, bps, c_heads, c_dh),
            lambda b, h, i, pt: (step_of(b, h, i) // steps_per_seq, step_of(b, h, i) % steps_per_seq, 0, 0)))
        out_shape.append(jax.ShapeDtypeStruct((n_dec, n_full, c_heads, c_dh), F32))
        args += [cache_k] * (bps * ppb)
        prefetch = [page_table]

    outs = pl.pallas_call(
        functools.partial(_moba_prompt_kernel, blk=blk, topk=MOBA_TOPK, scale=dh ** -0.5, grp=grp, hps=hps,
                          n_page_refs=bps * ppb if fused else 0, ppb=ppb if fused else 1),
        grid_spec=pltpu.PrefetchScalarGridSpec(
            num_scalar_prefetch=len(prefetch), grid=(n_seq, n_hp, nb),
            in_specs=in_specs, out_specs=out_specs,
            scratch_shapes=[pltpu.VMEM((hps, nb, dh, blk), BF16),
                            pltpu.VMEM((hps, nb, blk), F32),
                            pltpu.VMEM((hps, grp * blk, blk), F32),
                            pltpu.VMEM((hps, 1, blk), F32)]),
        out_shape=out_shape,
        compiler_params=_cparams("arbitrary", "arbitrary", "arbitrary"), name="moba_prompt",
    )(*prefetch, *args)
    return outs[0], (outs[1] if fused else None)


def _sample_topk_kernel(q_ref, mean_ref, o_ref, *, heads, topk):
    t_n = q_ref.shape[1]
    dh = q_ref.shape[2] // heads
    nb = mean_ref.shape[1]
    lane = lax.broadcasted_iota(jnp.int32, (t_n, LANES), 1)
    bidx = lax.broadcasted_iota(jnp.int32, (t_n, nb), 1)
    out = jnp.zeros((t_n, LANES), jnp.int32)
    for h in range(heads):
        qh = q_ref[0, :, h * dh:(h + 1) * dh]
        mh = mean_ref[0, :, h * dh:(h + 1) * dh]
        gate = _mm(qh, mh, NT_DIMS, precise=True)
        cnt = jnp.zeros((t_n, nb), jnp.int32)
        for jp in range(nb):
            colv = gate[:, jp:jp + 1]
            beats = (colv > gate) | ((colv == gate) & (jp < bidx))
            cnt = cnt + beats.astype(jnp.int32)
        for r in range(topk):
            idx_r = jnp.sum(jnp.where(cnt == r, bidx, 0), axis=1, keepdims=True)
            out = jnp.where(lane == h * topk + r, idx_r, out)
    o_ref[0] = out


def _sample_topk(q3, means3):
    n_seq, t_n, w = q3.shape
    nb = means3.shape[1]
    return pl.pallas_call(
        functools.partial(_sample_topk_kernel, heads=ATT_HEADS, topk=MOBA_TOPK),
        grid=(n_seq,),
        in_specs=[pl.BlockSpec((1, t_n, w), lambda b: (b, 0, 0)),
                  pl.BlockSpec((1, nb, w), lambda b: (b, 0, 0))],
        out_specs=pl.BlockSpec((1, t_n, LANES), lambda b: (b, 0, 0)),
        out_shape=jax.ShapeDtypeStruct((n_seq, t_n, LANES), jnp.int32),
        compiler_params=_cparams("parallel"), name="sample_topk",
    )(q3, means3)


def _moba_sample_kernel(idx_ref, pt_ref, q_ref, kn_ref, vn_ref, slope_ref, *refs,
                        heads, topk, ppb, past_len, blk, scale):
    del pt_ref
    t_n = q_ref.shape[1]
    n_slots = t_n * topk * ppb
    k_refs, v_refs, o_ref = refs[:n_slots], refs[n_slots:2 * n_slots], refs[2 * n_slots]
    ps, dh = k_refs[0].shape[0], k_refs[0].shape[2]
    b = pl.program_id(0)
    h = pl.program_id(1)
    q = q_ref[0]
    slope = slope_ref[0][:, :1]
    row_i = lax.broadcasted_iota(jnp.int32, (t_n, ps), 0)
    lane_i = lax.broadcasted_iota(jnp.int32, (t_n, ps), 1)

    scores = []
    for t in range(t_n):
        for r in range(topk):
            blk_idx = idx_ref[((b * t_n + t) * heads + h) * topk + r]
            for p in range(ppb):
                kp = k_refs[(t * topk + r) * ppb + p].reshape(ps, dh)[...]
                s = lax.dot_general(q, kp, NT_DIMS, preferred_element_type=F32) * scale
                dist0 = past_len + t - blk_idx * blk - p * ps
                s = s - slope * (dist0 - lane_i).astype(F32)
                scores.append(jnp.where(row_i == t, s, MASKED))

    ro = lax.broadcasted_iota(jnp.int32, (t_n, t_n), 0)
    co = lax.broadcasted_iota(jnp.int32, (t_n, t_n), 1)
    s_own = lax.dot_general(q, kn_ref[0], NT_DIMS, preferred_element_type=F32) * scale
    s_own = s_own - slope * (ro - co).astype(F32)
    s_own = jnp.where(co <= ro, s_own, MASKED)

    m_el = scores[0]
    for s in scores[1:]:
        m_el = jnp.maximum(m_el, s)
    m = jnp.maximum(jnp.max(m_el, axis=1, keepdims=True), jnp.max(s_own, axis=1, keepdims=True))
    p_own = jnp.exp(s_own - m)
    acc = jnp.dot(p_own, vn_ref[0], preferred_element_type=F32)
    l_el = jnp.zeros((t_n, ps), F32)
    for slot, s in enumerate(scores):
        pe = jnp.exp(s - m)
        l_el = l_el + pe
        acc = acc + jnp.dot(pe, v_refs[slot].reshape(ps, dh)[...], preferred_element_type=F32)
    l = jnp.sum(l_el, axis=1, keepdims=True) + jnp.sum(p_own, axis=1, keepdims=True)
    o_ref[0] = acc / l


def _moba_sample(q3, kn3, vn3, cache_k, cache_v, page_table, idx_flat, slopes, *, past_len):
    n_seq, t_n, w = q3.shape
    n_pool, ps, heads, dh = cache_k.shape
    ppb = MOBA_BLOCK // ps
    ck = cache_k.reshape(n_pool, ps, heads, 1, dh)
    cv = cache_v.reshape(n_pool, ps, heads, 1, dh)

    def page_map(b, h, idx, pt, *, t, r, p):
        blk_idx = idx[((b * t_n + t) * heads + h) * MOBA_TOPK + r]
        return (pt[b, blk_idx * ppb + p], 0, h, 0, 0)

    slot_specs = [pl.BlockSpec((None, ps, None, 1, dh), functools.partial(page_map, t=t, r=r, p=p))
                  for t in range(t_n) for r in range(MOBA_TOPK) for p in range(ppb)]
    tok_spec = pl.BlockSpec((1, t_n, dh), lambda b, h, idx, pt: (b, 0, h))
    n_slots = len(slot_specs)
    return pl.pallas_call(
        functools.partial(_moba_sample_kernel, heads=heads, topk=MOBA_TOPK, ppb=ppb, past_len=past_len,
                          blk=MOBA_BLOCK, scale=dh ** -0.5),
        grid_spec=pltpu.PrefetchScalarGridSpec(
            num_scalar_prefetch=2, grid=(n_seq, heads),
            in_specs=[tok_spec, tok_spec, tok_spec,
                      pl.BlockSpec((1, 1, LANES), lambda b, h, idx, pt: (h, 0, 0))] + slot_specs + slot_specs,
            out_specs=tok_spec),
        out_shape=jax.ShapeDtypeStruct((n_seq, t_n, w), F32),
        compiler_params=_cparams("parallel", "arbitrary"), name="moba_sample",
    )(idx_flat, page_table, q3, kn3, vn3, slopes, *([ck] * n_slots), *([cv] * n_slots))


def _alibi_slope_rows():
    s = jnp.float32(2.0) ** (-8.0 * (jnp.arange(ATT_HEADS, dtype=F32) + 1.0) / ATT_HEADS)
    return jnp.broadcast_to(s[:, None, None], (ATT_HEADS, 1, LANES))


def _decoder(x, retention_fn, attention_fn, w, depth, is_prompt):
    n_seq, seq, dm = x.shape
    m = n_seq * seq
    n_ret = depth // 2
    alpha = (2 * depth) ** 0.25
    hq = w["ret_in"][0].shape[1] // 6
    dk = hq // RET_HEADS
    aw = w["kv"].shape[1] // 2
    h = x.reshape(m, dm)
    states = []
    k_f32 = v_f32 = k_b = v_b = None
    precise = not is_prompt
    act = F32 if precise else BF16
    for layer in range(depth):
        if layer < n_ret:
            segs = ((0, hq, 1.0, (act,)), (hq, hq, dk ** -0.5, (act,)),
                    (2 * hq, 2 * hq, 1.0, (act,)), (4 * hq, 2 * hq, 1.0, (act,)))
            q, k, v, g = _proj(h, w["ret_in"][layer], segs, precise)
            mix_in, s_new = retention_fn(layer, q, k, v, g)
            states.append(s_new)
            w_mix, mix_layer = w["ret_out"], layer
        else:
            j = layer - n_ret
            if j == 0:
                k_dt = (F32, BF16) if is_prompt else (F32,)
                q_dt = (BF16,) if is_prompt else (F32,)
                segs = ((0, aw, 1.0, k_dt), (aw, aw, 1.0, k_dt), (2 * aw, aw, 1.0, q_dt))
                outs = _proj(h, jnp.concatenate([w["kv"], w["q"][j]], axis=1), segs, precise,
                             means_blk=MOBA_BLOCK if is_prompt else 0)
                if is_prompt:
                    k_f32, k_b, v_f32, v_b, qa, k_means = outs
                    k_means = k_means.reshape(m // MOBA_BLOCK, aw)
                else:
                    k_f32, v_f32, qa = outs
                    k_means = None
            else:
                (qa,) = _proj(h, w["q"][j], ((0, aw, 1.0, (BF16,) if is_prompt else (F32,)),), precise)
            mix_in = attention_fn(j, qa, k_f32, v_f32, k_b, v_b, k_means)
            w_mix, mix_layer = w["att_out"], j
        h = _outproj_ln(mix_in, w_mix, mix_layer, h, w["ln_g"][layer, 0], w["ln_b"][layer, 0], alpha, precise)
        f = _ffn_in(h, w["ffn_in"], layer, precise)
        h = _outproj_ln(f, w["ffn_out"], layer, h, w["ln_g"][layer, 1], w["ln_b"][layer, 1], alpha, precise)
    return h.reshape(n_seq, seq, dm), jnp.stack(states), k_f32, v_f32


def kernel(x_prompt, x_sample, cache_k, cache_v, state_ret, page_table, w_ret_in, w_ret_out, w_kv,
           w_q, w_att_out, w_ffn_in, w_ffn_out, ln_g, ln_b):
    depth = w_ffn_in.shape[0]
    n_b, seq, _ = x_prompt.shape
    n_d, t_n, _ = x_sample.shape
    ps = cache_k.shape[1]
    n_pages = page_table.shape[1]
    past_len = n_pages * ps
    ppb = MOBA_BLOCK // ps
    n_full = past_len // MOBA_BLOCK
    assert seq % MOBA_BLOCK == 0 and seq % RET_CHUNK == 0
    assert n_full * ppb == n_pages and n_full >= MOBA_TOPK
    assert t_n <= SAMPLE_CHUNK_PAD

    w32 = {"ret_in": w_ret_in, "ret_out": w_ret_out, "kv": w_kv, "q": w_q, "att_out": w_att_out,
           "ffn_in": w_ffn_in, "ffn_out": w_ffn_out, "ln_g": ln_g, "ln_b": ln_b}
    w = {name: (a if name.startswith("ln_") else a.astype(BF16)) for name, a in w32.items()}
    slopes = _alibi_slope_rows()
    cache_means = [None]

    def ret_prompt(layer, q, k, v, g):
        del layer
        c3 = lambda a: a.reshape(-1, RET_CHUNK, a.shape[-1])
        o, s = _retention(c3(q), c3(k), c3(v), c3(g), None, n_seq=n_b, c_real=RET_CHUNK)
        return o.reshape(n_b * seq, -1), s

    def att_prompt(j, q, k_f32, v_f32, k_b, v_b, means):
        del k_f32, v_f32
        if j == 0:
            o, cache_means[0] = _moba_prompt(q, k_b, v_b, means, slopes, n_seq=n_b,
                                             cache_k=cache_k, page_table=page_table)
        else:
            o, _ = _moba_prompt(q, k_b, v_b, means, slopes, n_seq=n_b)
        return o

    y_p, s_p, k_p, v_p = _decoder(x_prompt, ret_prompt, att_prompt, w, depth, True)

    def ret_sample(layer, q, k, v, g):
        def c3(a):
            a = a.reshape(n_d, t_n, a.shape[-1])
            return jnp.pad(a, ((0, 0), (0, SAMPLE_CHUNK_PAD - t_n), (0, 0)))
        o, s = _retention(c3(q), c3(k), c3(v), c3(g), state_ret[layer].astype(F32), n_seq=n_d, c_real=t_n,
                          precise=True)
        return o[:, :t_n].reshape(n_d * t_n, -1), s

    def att_sample(j, q, k_f32, v_f32, k_b, v_b, means):
        del j, k_b, v_b, means
        r3 = lambda a: a.reshape(n_d, t_n, a.shape[-1])
        if cache_means[0] is None:
            cache_means[0] = _cache_means(cache_k, page_table, n_full, ppb)
        idx = _sample_topk(r3(q), cache_means[0].reshape(n_d, n_full, -1))
        idx_flat = idx[:, :, :ATT_HEADS * MOBA_TOPK].reshape(-1)
        o = _moba_sample(r3(q), r3(k_f32), r3(v_f32), cache_k, cache_v, page_table, idx_flat, slopes,
                         past_len=past_len)
        return o.reshape(n_d * t_n, -1)

    y_s, s_s, k_s, v_s = _decoder(x_sample, ret_sample, att_sample, w32, depth, False)

    dh = k_p.shape[-1] // ATT_HEADS
    kv4 = lambda a, n, l: a.reshape(n, l, ATT_HEADS, dh)
    return (y_p, y_s, s_p, s_s, kv4(k_p, n_b, seq), kv4(v_p, n_b, seq), kv4(k_s, n_d, t_n), kv4(v_s, n_d, t_n))
```

```python
import functools
import math

import jax
import jax.numpy as jnp
from jax import lax
from jax.experimental import pallas as pl
from jax.experimental.pallas import tpu as pltpu

F32 = jnp.float32
BF16 = jnp.bfloat16

RET_HEADS = 4
RET_CHUNK = 128
ATT_HEADS = 8
MOBA_BLOCK = 256
MOBA_TOPK = 3
PAGE_SIZE = 128
LN_EPS = 1e-5
GN_EPS = 1e-6
MASKED = -1e30
V7X_VMEM_LIMIT_BYTES = 56 * 1024 * 1024
SAMPLE_CHUNK_PAD = 16
LANES = 128
MOBA_KV_GROUP = 4
MOBA_HEADS_PER_STEP = 2

NT_DIMS = (((1,), (1,)), ((), ()))
TN_DIMS = (((0,), (0,)), ((), ()))


def _cparams(*sem):
    return pltpu.CompilerParams(dimension_semantics=sem, vmem_limit_bytes=V7X_VMEM_LIMIT_BYTES)


def _row_tile(m):
    for t in (512, 256, 128, 64, 32, 16, 8):
        if m % t == 0:
            return t
    return m


def _mm(a, b, dims=None, *, precise=False):
    dims = dims or (((a.ndim - 1,), (0,)), ((), ()))
    if precise:
        return lax.dot_general(a.astype(F32), b.astype(F32), dims, precision=lax.Precision.HIGHEST,
                               preferred_element_type=F32)
    return lax.dot_general(a.astype(BF16), b.astype(BF16), dims, preferred_element_type=F32)


def _proj_kernel(x_ref, w_ref, *o_refs, segs, chunk, precise, means_blk):
    xb = x_ref[...] if precise else x_ref[...].astype(BF16)
    oi = 0
    for si, (start, width, scale, dtypes) in enumerate(segs):
        for c0 in range(0, width, chunk):
            cw = min(chunk, width - c0)
            acc = _mm(xb, w_ref[:, start + c0:start + c0 + cw], precise=precise)
            if scale != 1.0:
                acc = acc * scale
            for d, dt in enumerate(dtypes):
                o_refs[oi + d][:, c0:c0 + cw] = acc.astype(dt)
            if means_blk and si == 0:
                n_blk = acc.shape[0] // means_blk
                o_refs[-1][:, 0, c0:c0 + cw] = jnp.mean(acc.reshape(n_blk, means_blk, cw), axis=1)
        oi += len(dtypes)


def _proj(x, w, segs, precise=False, means_blk=0):
    m, kdim = x.shape
    tm = _row_tile(m)
    if precise:
        outs = []
        for start, width, scale, dtypes in segs:
            assert start % width == 0
            outs += pl.pallas_call(
                functools.partial(_proj_kernel, segs=((0, width, scale, dtypes),), chunk=512, precise=True,
                                  means_blk=0),
                grid=(m // tm,),
                in_specs=[pl.BlockSpec((tm, kdim), lambda i: (i, 0)),
                          pl.BlockSpec((kdim, width), functools.partial(lambda i, c: (0, c), c=start // width))],
                out_specs=[pl.BlockSpec((tm, width), lambda i: (i, 0)) for _ in dtypes],
                out_shape=[jax.ShapeDtypeStruct((m, width), dt) for dt in dtypes],
                compiler_params=_cparams("parallel"), name="proj_precise",
            )(x, w)
        return outs
    out_shape, out_specs = [], []
    for _, width, _, dtypes in segs:
        for dt in dtypes:
            out_shape.append(jax.ShapeDtypeStruct((m, width), dt))
            out_specs.append(pl.BlockSpec((tm, width), lambda i: (i, 0)))
    if means_blk:
        assert tm % means_blk == 0
        out_shape.append(jax.ShapeDtypeStruct((m // means_blk, 1, segs[0][1]), F32))
        out_specs.append(pl.BlockSpec((tm // means_blk, 1, segs[0][1]), lambda i: (i, 0, 0)))
    return pl.pallas_call(
        functools.partial(_proj_kernel, segs=segs, chunk=512, precise=False, means_blk=means_blk),
        grid=(m // tm,),
        in_specs=[pl.BlockSpec((tm, kdim), lambda i: (i, 0)),
                  pl.BlockSpec(w.shape, lambda i: (0, 0))],
        out_specs=out_specs, out_shape=out_shape,
        compiler_params=_cparams("parallel"), name="proj",
    )(x, w)


def _outproj_ln_kernel(a_ref, w_ref, r_ref, g_ref, b_ref, o_ref, *, alpha, precise):
    acc = _mm(a_ref[...], w_ref[...], precise=precise)
    y = alpha * r_ref[...] + acc
    mu = jnp.mean(y, axis=-1, keepdims=True)
    d = y - mu
    var = jnp.mean(d * d, axis=-1, keepdims=True)
    o_ref[...] = d * lax.rsqrt(var + LN_EPS) * g_ref[...] + b_ref[...]


def _outproj_ln(a, w, layer, res, g, b, alpha, precise=False):
    m, kdim = a.shape
    dm = w.shape[2]
    tm = _row_tile(m)
    return pl.pallas_call(
        functools.partial(_outproj_ln_kernel, alpha=alpha, precise=precise),
        grid=(m // tm,),
        in_specs=[pl.BlockSpec((tm, kdim), lambda i: (i, 0)),
                  pl.BlockSpec((None, kdim, dm), lambda i: (layer, 0, 0)),
                  pl.BlockSpec((tm, dm), lambda i: (i, 0)),
                  pl.BlockSpec((1, dm), lambda i: (0, 0)),
                  pl.BlockSpec((1, dm), lambda i: (0, 0))],
        out_specs=pl.BlockSpec((tm, dm), lambda i: (i, 0)),
        out_shape=jax.ShapeDtypeStruct((m, dm), F32),
        compiler_params=_cparams("parallel"), name="outproj_ln",
    )(a, w, res, g.reshape(1, dm), b.reshape(1, dm))


def _ffn_in_kernel(h_ref, w_ref, o_ref, *, dff, chunk):
    hb = h_ref[...].astype(BF16)
    for c0 in range(0, dff, chunk):
        cw = min(chunk, dff - c0)
        gate = jnp.dot(hb, w_ref[:, c0:c0 + cw], preferred_element_type=F32)
        up = jnp.dot(hb, w_ref[:, dff + c0:dff + c0 + cw], preferred_element_type=F32)
        o_ref[:, c0:c0 + cw] = (gate * jax.nn.sigmoid(gate) * up).astype(o_ref.dtype)


def _ffn_in_chunk_kernel(h_ref, wg_ref, wu_ref, o_ref):
    gate = _mm(h_ref[...], wg_ref[...], precise=True)
    up = _mm(h_ref[...], wu_ref[...], precise=True)
    o_ref[...] = gate * jax.nn.sigmoid(gate) * up


def _ffn_in(h, w, layer, precise=False):
    m, dm = h.shape
    dff = w.shape[2] // 2
    tm = _row_tile(m)
    if precise:
        tn = 256
        assert dff % tn == 0 and m == tm
        n_chunks = dff // tn
        return pl.pallas_call(
            _ffn_in_chunk_kernel,
            grid=(n_chunks,),
            in_specs=[pl.BlockSpec((m, dm), lambda c: (0, 0)),
                      pl.BlockSpec((None, dm, tn), lambda c: (layer, 0, c)),
                      pl.BlockSpec((None, dm, tn), lambda c: (layer, 0, n_chunks + c))],
            out_specs=pl.BlockSpec((m, tn), lambda c: (0, c)),
            out_shape=jax.ShapeDtypeStruct((m, dff), F32),
            compiler_params=_cparams("parallel"), name="ffn_in_precise",
        )(h, w, w)
    return pl.pallas_call(
        functools.partial(_ffn_in_kernel, dff=dff, chunk=512),
        grid=(m // tm,),
        in_specs=[pl.BlockSpec((tm, dm), lambda i: (i, 0)),
                  pl.BlockSpec((None,) + w.shape[1:], lambda i: (layer, 0, 0))],
        out_specs=pl.BlockSpec((tm, dff), lambda i: (i, 0)),
        out_shape=jax.ShapeDtypeStruct((m, dff), BF16),
        compiler_params=_cparams("parallel"), name="ffn_in",
    )(h, w)


def _retention_kernel(*refs, heads, c_real, has_init, precise):
    if has_init:
        q_ref, k_ref, v_ref, g_ref, s0_ref, o_ref, s_ref = refs
    else:
        q_ref, k_ref, v_ref, g_ref, o_ref, s_ref = refs

    @pl.when(pl.program_id(1) == 0)
    def _():
        if has_init:
            s_ref[...] = s0_ref[...]
        else:
            s_ref[...] = jnp.zeros(s_ref.shape, F32)

    cp = q_ref.shape[1]
    dk = q_ref.shape[2] // heads
    dv = v_ref.shape[2] // heads
    row = lax.broadcasted_iota(jnp.int32, (cp, cp), 0)
    col = lax.broadcasted_iota(jnp.int32, (cp, cp), 1)
    diff = (row - col).astype(F32)
    ri = lax.broadcasted_iota(jnp.int32, (cp, 1), 0).astype(F32)
    for h in range(heads):
        lg = math.log(1.0 - 2.0 ** (-5.0 - h))
        decay = jnp.where(diff >= 0, jnp.exp(jnp.maximum(diff, 0.0) * lg), 0.0)
        xi = jnp.exp((ri + 1.0) * lg)
        zeta = jnp.exp((c_real - 1.0 - ri) * lg)
        qh = q_ref[0, :, h * dk:(h + 1) * dk]
        kh = k_ref[0, :, h * dk:(h + 1) * dk]
        vh = v_ref[0, :, h * dv:(h + 1) * dv]
        s_old = s_ref[0, h]
        scores = _mm(qh, kh, NT_DIMS, precise=precise) * decay
        intra = _mm(scores, vh, precise=precise)
        cross = _mm(qh, s_old, precise=precise) * xi
        upd = _mm(kh.astype(F32) * zeta, vh, TN_DIMS, precise=precise)
        s_ref[0, h] = math.exp(c_real * lg) * s_old + upd
        o = intra + cross
        mu = jnp.mean(o, axis=-1, keepdims=True)
        d = o - mu
        var = jnp.mean(d * d, axis=-1, keepdims=True)
        on = d * lax.rsqrt(var + GN_EPS)
        gh = g_ref[0, :, h * dv:(h + 1) * dv].astype(F32)
        o_ref[0, :, h * dv:(h + 1) * dv] = (gh * jax.nn.sigmoid(gh) * on).astype(o_ref.dtype)


def _retention(q, k, v, g, s0, *, n_seq, c_real, precise=False):
    n_chunks, cp, qw = q.shape
    vw = v.shape[2]
    nc = n_chunks // n_seq
    dk, dv = qw // RET_HEADS, vw // RET_HEADS
    tok_map = lambda b, c: (b * nc + c, 0, 0)
    state_spec = pl.BlockSpec((1, RET_HEADS, dk, dv), lambda b, c: (b, 0, 0, 0))
    in_specs = [pl.BlockSpec((1, cp, qw), tok_map), pl.BlockSpec((1, cp, qw), tok_map),
                pl.BlockSpec((1, cp, vw), tok_map), pl.BlockSpec((1, cp, vw), tok_map)]
    args = [q, k, v, g]
    if s0 is not None:
        in_specs.append(state_spec)
        args.append(s0)
    return pl.pallas_call(
        functools.partial(_retention_kernel, heads=RET_HEADS, c_real=c_real, has_init=s0 is not None,
                          precise=precise),
        grid=(n_seq, nc),
        in_specs=in_specs,
        out_specs=[pl.BlockSpec((1, cp, vw), tok_map), state_spec],
        out_shape=[jax.ShapeDtypeStruct((n_chunks, cp, vw), F32 if precise else BF16),
                   jax.ShapeDtypeStruct((n_seq, RET_HEADS, dk, dv), F32)],
        compiler_params=_cparams("parallel", "arbitrary"), name="retention",
    )(*args)


def _page_block_means(page_refs, o_ref, ppb):
    for gi in range(len(page_refs) // ppb):
        acc = jnp.sum(page_refs[gi * ppb][0], axis=0)
        for p in range(1, ppb):
            acc = acc + jnp.sum(page_refs[gi * ppb + p][0], axis=0)
        o_ref[0, gi] = acc * (1.0 / (ppb * page_refs[0].shape[1]))


def _cache_means_kernel(pt_ref, *refs, ppb):
    del pt_ref
    _page_block_means(refs[:-1], refs[-1], ppb)


def _cache_means(cache_k, page_table, n_full, ppb):
    n_seq = page_table.shape[0]
    _, ps, heads, dh = cache_k.shape
    grp = 4 if n_full % 4 == 0 else 1

    def page_map(b, n, pt, *, gi, p):
        return (pt[b, (n * grp + gi) * ppb + p], 0, 0, 0)

    in_specs = [pl.BlockSpec((1, ps, heads, dh), functools.partial(page_map, gi=gi, p=p))
                for gi in range(grp) for p in range(ppb)]
    return pl.pallas_call(
        functools.partial(_cache_means_kernel, ppb=ppb),
        grid_spec=pltpu.PrefetchScalarGridSpec(
            num_scalar_prefetch=1, grid=(n_seq, n_full // grp), in_specs=in_specs,
            out_specs=pl.BlockSpec((1, grp, heads, dh), lambda b, n, pt: (b, n, 0, 0))),
        out_shape=jax.ShapeDtypeStruct((n_seq, n_full, heads, dh), F32),
        compiler_params=_cparams("parallel", "arbitrary"), name="cache_means",
    )(page_table, *([cache_k] * (grp * ppb)))


def _rank_rows(g):
    n = g.shape[0]
    sub = 8
    sub_i = lax.broadcasted_iota(jnp.int32, (sub, g.shape[1]), 0)
    groups = []
    for r0 in range(0, n, sub):
        gr = g[r0:r0 + sub, :]
        cnt = jnp.zeros(gr.shape, jnp.int32)
        for jp in range(n):
            row = g[jp:jp + 1, :]
            if jp < r0:
                inc = jnp.where(row >= gr, 1, 0)
            elif jp >= r0 + gr.shape[0] - 1:
                inc = jnp.where(row > gr, 1, 0)
            else:
                inc = jnp.where(sub_i[:gr.shape[0]] > jp - r0,
                                jnp.where(row >= gr, 1, 0), jnp.where(row > gr, 1, 0))
            cnt = cnt + inc
        groups.append(cnt)
    return jnp.concatenate(groups, axis=0)


def _moba_prompt_kernel(*refs, blk, topk, scale, grp, hps, n_page_refs, ppb):
    if n_page_refs:
        refs = refs[1:]
    q_ref, k_ref, v_ref, mean_ref, slope_ref = refs[:5]
    page_refs = refs[5:5 + n_page_refs]
    refs = refs[5 + n_page_refs:]
    if n_page_refs:
        o_ref, cmean_ref = refs[:2]
        refs = refs[2:]
    else:
        o_ref = refs[0]
        refs = refs[1:]
    vt_ref, sel_ref, u_ref, mx_ref = refs

    i = pl.program_id(2)
    nb = k_ref.shape[0] // blk
    dh = q_ref.shape[1] // hps
    inv_scale = 1.0 / scale
    exp2_coeff = scale * math.log2(math.e)
    hcols = lambda hh: slice(hh * dh, (hh + 1) * dh)

    @pl.when(i == 0)
    def _():
        for hh in range(hps):
            for jb in range(nb):
                vt_ref[hh, jb] = v_ref[jb * blk:(jb + 1) * blk, hcols(hh)].astype(F32).T.astype(BF16)

    if n_page_refs:
        _page_block_means(page_refs, cmean_ref, ppb)

    key_i = lax.broadcasted_iota(jnp.int32, (blk, blk), 0)
    qry_i = lax.broadcasted_iota(jnp.int32, (blk, blk), 1)
    dq = (qry_i - key_i).astype(F32)
    bidx = lax.broadcasted_iota(jnp.int32, (nb, blk), 0)
    past = bidx < i
    blk_dist = ((i - bidx) * blk).astype(F32)

    qts, dbiases, state = [], [], []
    for hh in range(hps):
        slope_u = slope_ref[hh][:, :1] * inv_scale
        qt = q_ref[:, hcols(hh)].astype(F32).T.astype(BF16)
        gate = jnp.dot(mean_ref[:, hcols(hh)].astype(BF16), qt, preferred_element_type=F32)
        gm = jnp.where(past, gate, -jnp.inf)
        cnt = _rank_rows(gm)
        sel_ref[hh] = jnp.where(past & (cnt < topk), 0.0, MASKED) - blk_dist * slope_u
        dbias = dq * slope_u
        k_own = k_ref[pl.ds(pl.multiple_of(i * blk, blk), blk), hcols(hh)]
        u = jnp.dot(k_own, qt, preferred_element_type=F32) - dbias
        u = jnp.where(dq >= 0, u, MASKED)
        m0 = jnp.max(u, axis=0, keepdims=True)
        p = jnp.exp2((u - m0) * exp2_coeff)
        l0 = jnp.sum(p, axis=0, keepdims=True)
        acc0 = jnp.dot(vt_ref[hh, i], p.astype(BF16), preferred_element_type=F32)
        qts.append(qt)
        dbiases.append(dbias)
        state.append((m0, l0, acc0))

    def stage_a(hh, j0):
        mx = None
        for g in range(grp):
            kj = k_ref[pl.ds(pl.multiple_of((j0 + g) * blk, blk), blk), hcols(hh)]
            u = jnp.dot(kj, qts[hh], preferred_element_type=F32) - dbiases[hh] + sel_ref[hh, pl.ds(j0 + g, 1), :]
            u_ref[hh, g * blk:(g + 1) * blk, :] = u
            mx = u if mx is None else jnp.maximum(mx, u)
        mx_ref[hh] = jnp.max(mx, axis=0, keepdims=True)

    def stage_b(hh, j0, m_old, l_old, acc_old):
        m_new = jnp.maximum(m_old, mx_ref[hh])
        alpha = jnp.exp2((m_old - m_new) * exp2_coeff)
        acc = alpha * acc_old
        l_el = None
        for g in range(grp):
            pg = jnp.exp2((u_ref[hh, g * blk:(g + 1) * blk, :] - m_new) * exp2_coeff)
            l_el = pg if l_el is None else l_el + pg
            acc = acc + jnp.dot(vt_ref[hh, j0 + g], pg.astype(BF16), preferred_element_type=F32)
        return m_new, alpha * l_old + jnp.sum(l_el, axis=0, keepdims=True), acc

    n_grp = jnp.maximum((i + grp - 1) // grp, 1)
    for hh in range(hps):
        stage_a(hh, 0)

    def trip(gi, carry):
        out = tuple(stage_b(hh, gi * grp, *carry[hh]) for hh in range(hps))
        for hh in range(hps):
            stage_a(hh, (gi + 1) * grp)
        return out

    state = lax.fori_loop(0, n_grp - 1, trip, tuple(state))
    for hh in range(hps):
        _, l_f, acc_f = stage_b(hh, (n_grp - 1) * grp, *state[hh])
        o_ref[:, hcols(hh)] = (acc_f / l_f).T.astype(o_ref.dtype)


def _moba_prompt(q, k, v, means, slopes, *, n_seq, cache_k=None, page_table=None):
    m, w = q.shape
    seq = m // n_seq
    dh = w // ATT_HEADS
    blk = MOBA_BLOCK
    nb = seq // blk
    hps = MOBA_HEADS_PER_STEP if ATT_HEADS % MOBA_HEADS_PER_STEP == 0 else 1
    grp = MOBA_KV_GROUP if nb % MOBA_KV_GROUP == 0 else 1
    n_hp = ATT_HEADS // hps
    n_steps = n_seq * n_hp * nb

    bps = 0
    if cache_k is not None:
        n_dec, n_pages = page_table.shape
        _, ps, c_heads, c_dh = cache_k.shape
        ppb = blk // ps
        n_full = n_pages // ppb
        if (n_dec * n_full) % n_steps == 0 and n_full % ((n_dec * n_full) // n_steps) == 0:
            bps = (n_dec * n_full) // n_steps
    fused = bps > 0

    def tok_map(b, h, i, *_):
        return (b * nb + i, h)

    def seq_map(b, h, i, *_):
        return (b, h)

    in_specs = [pl.BlockSpec((blk, hps * dh), tok_map),
                pl.BlockSpec((seq, hps * dh), seq_map),
                pl.BlockSpec((seq, hps * dh), seq_map),
                pl.BlockSpec((nb, hps * dh), seq_map),
                pl.BlockSpec((hps, 1, LANES), lambda b, h, i, *_: (h, 0, 0))]
    out_specs = [pl.BlockSpec((blk, hps * dh), tok_map)]
    out_shape = [jax.ShapeDtypeStruct((m, w), BF16)]
    args = [q, k, v, means, slopes]
    prefetch = []
    if fused:
        steps_per_seq = n_full // bps

        def step_of(b, h, i):
            return (b * n_hp + h) * nb + i

        def page_map(b, h, i, pt, *, n):
            s = step_of(b, h, i)
            return (pt[s // steps_per_seq, (s % steps_per_seq) * bps * ppb + n], 0, 0, 0)

        in_specs += [pl.BlockSpec((1, ps, c_heads, c_dh), functools.partial(page_map, n=n))
                     for n in range(bps * ppb)]
        out_specs.append(pl.BlockSpec(
            (1, bps, c_heads, c_dh),
            lambda b, h, i, pt: (step_of(b, h, i) // steps_per_seq, step_of(b, h, i) % steps_per_seq, 0, 0)))
        out_shape.append(jax.ShapeDtypeStruct((n_dec, n_full, c_heads, c_dh), F32))
        args += [cache_k] * (bps * ppb)
        prefetch = [page_table]

    outs = pl.pallas_call(
        functools.partial(_moba_prompt_kernel, blk=blk, topk=MOBA_TOPK, scale=dh ** -0.5, grp=grp, hps=hps,
                          n_page_refs=bps * ppb if fused else 0, ppb=ppb if fused else 1),
        grid_spec=pltpu.PrefetchScalarGridSpec(
            num_scalar_prefetch=len(prefetch), grid=(n_seq, n_hp, nb),
            in_specs=in_specs, out_specs=out_specs,
            scratch_shapes=[pltpu.VMEM((hps, nb, dh, blk), BF16),
                            pltpu.VMEM((hps, nb, blk), F32),
                            pltpu.VMEM((hps, grp * blk, blk), F32),
                            pltpu.VMEM((hps, 1, blk), F32)]),
        out_shape=out_shape,
        compiler_params=_cparams("arbitrary", "arbitrary", "arbitrary"), name="moba_prompt",
    )(*prefetch, *args)
    return outs[0], (outs[1] if fused else None)


def _sample_topk_kernel(q_ref, mean_ref, pt_ref, o_ref, p_ref, *, heads, topk, ppb):
    t_n = q_ref.shape[1]
    dh = q_ref.shape[2] // heads
    nb = mean_ref.shape[1]
    n_pages = pt_ref.shape[2]
    lane = lax.broadcasted_iota(jnp.int32, (t_n, LANES), 1)
    bidx = lax.broadcasted_iota(jnp.int32, (t_n, nb), 1)
    page_i = lax.broadcasted_iota(jnp.int32, (t_n, n_pages), 1)
    pt_row = pt_ref[0]
    out = jnp.zeros((t_n, LANES), jnp.int32)
    phys = jnp.zeros((t_n, LANES), jnp.int32)
    for h in range(heads):
        qh = q_ref[0, :, h * dh:(h + 1) * dh]
        mh = mean_ref[0, :, h * dh:(h + 1) * dh]
        gate = _mm(qh, mh, NT_DIMS, precise=True)
        cnt = jnp.zeros((t_n, nb), jnp.int32)
        for jp in range(nb):
            colv = gate[:, jp:jp + 1]
            beats = (colv > gate) | ((colv == gate) & (jp < bidx))
            cnt = cnt + beats.astype(jnp.int32)
        for r in range(topk):
            idx_r = jnp.sum(jnp.where(cnt == r, bidx, 0), axis=1, keepdims=True)
            out = jnp.where(lane == h * topk + r, idx_r, out)
            for p in range(ppb):
                page = jnp.sum(jnp.where(page_i == idx_r * ppb + p, pt_row, 0), axis=1, keepdims=True)
                phys = jnp.where(lane == (h * topk + r) * ppb + p, page, phys)
    o_ref[0] = out
    p_ref[0] = phys


def _sample_topk(q3, means3, page_table, ppb):
    n_seq, t_n, w = q3.shape
    nb = means3.shape[1]
    n_pages = page_table.shape[1]
    assert ATT_HEADS * MOBA_TOPK * ppb <= LANES
    out = jax.ShapeDtypeStruct((n_seq, t_n, LANES), jnp.int32)
    out_spec = pl.BlockSpec((1, t_n, LANES), lambda b: (b, 0, 0))
    return pl.pallas_call(
        functools.partial(_sample_topk_kernel, heads=ATT_HEADS, topk=MOBA_TOPK, ppb=ppb),
        grid=(n_seq,),
        in_specs=[pl.BlockSpec((1, t_n, w), lambda b: (b, 0, 0)),
                  pl.BlockSpec((1, nb, w), lambda b: (b, 0, 0)),
                  pl.BlockSpec((1, 1, n_pages), lambda b: (b, 0, 0))],
        out_specs=[out_spec, out_spec],
        out_shape=[out, out],
        compiler_params=_cparams("parallel"), name="sample_topk",
    )(q3, means3, page_table.reshape(n_seq, 1, n_pages))


def _moba_sample_kernel(idx_ref, phys_ref, q_ref, kn_ref, vn_ref, slope_ref, *refs,
                        heads, topk, ppb, past_len, blk, scale):
    del phys_ref
    t_n = q_ref.shape[1]
    n_slots = t_n * topk * ppb
    k_refs, v_refs, o_ref = refs[:n_slots], refs[n_slots:2 * n_slots], refs[2 * n_slots]
    ps, dh = k_refs[0].shape[0], k_refs[0].shape[2]
    b = pl.program_id(0)
    h = pl.program_id(1)
    q = q_ref[0]
    slope = slope_ref[0][:, :1]
    row_i = lax.broadcasted_iota(jnp.int32, (t_n, ps), 0)
    lane_i = lax.broadcasted_iota(jnp.int32, (t_n, ps), 1)

    scores = []
    for t in range(t_n):
        for r in range(topk):
            blk_idx = idx_ref[((b * t_n + t) * heads + h) * topk + r]
            for p in range(ppb):
                kp = k_refs[(t * topk + r) * ppb + p].reshape(ps, dh)[...]
                s = lax.dot_general(q, kp, NT_DIMS, preferred_element_type=F32) * scale
                dist0 = past_len + t - blk_idx * blk - p * ps
                s = s - slope * (dist0 - lane_i).astype(F32)
                scores.append(jnp.where(row_i == t, s, MASKED))

    ro = lax.broadcasted_iota(jnp.int32, (t_n, t_n), 0)
    co = lax.broadcasted_iota(jnp.int32, (t_n, t_n), 1)
    s_own = lax.dot_general(q, kn_ref[0], NT_DIMS, preferred_element_type=F32) * scale
    s_own = s_own - slope * (ro - co).astype(F32)
    s_own = jnp.where(co <= ro, s_own, MASKED)

    m_el = scores[0]
    for s in scores[1:]:
        m_el = jnp.maximum(m_el, s)
    m = jnp.maximum(jnp.max(m_el, axis=1, keepdims=True), jnp.max(s_own, axis=1, keepdims=True))
    p_own = jnp.exp(s_own - m)
    acc = jnp.dot(p_own, vn_ref[0], preferred_element_type=F32)
    l_el = jnp.zeros((t_n, ps), F32)
    for slot, s in enumerate(scores):
        pe = jnp.exp(s - m)
        l_el = l_el + pe
        acc = acc + jnp.dot(pe, v_refs[slot].reshape(ps, dh)[...], preferred_element_type=F32)
    l = jnp.sum(l_el, axis=1, keepdims=True) + jnp.sum(p_own, axis=1, keepdims=True)
    o_ref[0] = acc / l


def _moba_sample(q3, kn3, vn3, cache_k, cache_v, idx_flat, phys_flat, slopes, *, past_len):
    n_seq, t_n, w = q3.shape
    n_pool, ps, heads, dh = cache_k.shape
    ppb = MOBA_BLOCK // ps
    ck = cache_k.reshape(n_pool, ps, heads, 1, dh)
    cv = cache_v.reshape(n_pool, ps, heads, 1, dh)

    def page_map(b, h, idx, phys, *, t, r, p):
        del idx
        return (phys[(((b * t_n + t) * heads + h) * MOBA_TOPK + r) * ppb + p], 0, h, 0, 0)

    slot_specs = [pl.BlockSpec((None, ps, None, 1, dh), functools.partial(page_map, t=t, r=r, p=p))
                  for t in range(t_n) for r in range(MOBA_TOPK) for p in range(ppb)]
    tok_spec = pl.BlockSpec((1, t_n, dh), lambda b, h, idx, pt: (b, 0, h))
    n_slots = len(slot_specs)
    return pl.pallas_call(
        functools.partial(_moba_sample_kernel, heads=heads, topk=MOBA_TOPK, ppb=ppb, past_len=past_len,
                          blk=MOBA_BLOCK, scale=dh ** -0.5),
        grid_spec=pltpu.PrefetchScalarGridSpec(
            num_scalar_prefetch=2, grid=(n_seq, heads),
            in_specs=[tok_spec, tok_spec, tok_spec,
                      pl.BlockSpec((1, 1, LANES), lambda b, h, idx, pt: (h, 0, 0))] + slot_specs + slot_specs,
            out_specs=tok_spec),
        out_shape=jax.ShapeDtypeStruct((n_seq, t_n, w), F32),
        compiler_params=_cparams("parallel", "arbitrary"), name="moba_sample",
    )(idx_flat, phys_flat, q3, kn3, vn3, slopes, *([ck] * n_slots), *([cv] * n_slots))


def _alibi_slope_rows():
    s = jnp.float32(2.0) ** (-8.0 * (jnp.arange(ATT_HEADS, dtype=F32) + 1.0) / ATT_HEADS)
    return jnp.broadcast_to(s[:, None, None], (ATT_HEADS, 1, LANES))


def _decoder(x, retention_fn, attention_fn, w, depth, is_prompt):
    n_seq, seq, dm = x.shape
    m = n_seq * seq
    n_ret = depth // 2
    alpha = (2 * depth) ** 0.25
    hq = w["ret_in"][0].shape[1] // 6
    dk = hq // RET_HEADS
    aw = w["kv"].shape[1] // 2
    h = x.reshape(m, dm)
    states = []
    k_f32 = v_f32 = k_b = v_b = None
    precise = not is_prompt
    act = F32 if precise else BF16
    for layer in range(depth):
        if layer < n_ret:
            segs = ((0, hq, 1.0, (act,)), (hq, hq, dk ** -0.5, (act,)),
                    (2 * hq, 2 * hq, 1.0, (act,)), (4 * hq, 2 * hq, 1.0, (act,)))
            q, k, v, g = _proj(h, w["ret_in"][layer], segs, precise)
            mix_in, s_new = retention_fn(layer, q, k, v, g)
            states.append(s_new)
            w_mix, mix_layer = w["ret_out"], layer
        else:
            j = layer - n_ret
            if j == 0:
                k_dt = (F32, BF16) if is_prompt else (F32,)
                q_dt = (BF16,) if is_prompt else (F32,)
                segs = ((0, aw, 1.0, k_dt), (aw, aw, 1.0, k_dt), (2 * aw, aw, 1.0, q_dt))
                outs = _proj(h, jnp.concatenate([w["kv"], w["q"][j]], axis=1), segs, precise,
                             means_blk=MOBA_BLOCK if is_prompt else 0)
                if is_prompt:
                    k_f32, k_b, v_f32, v_b, qa, k_means = outs
                    k_means = k_means.reshape(m // MOBA_BLOCK, aw)
                else:
                    k_f32, v_f32, qa = outs
                    k_means = None
            else:
                (qa,) = _proj(h, w["q"][j], ((0, aw, 1.0, (BF16,) if is_prompt else (F32,)),), precise)
            mix_in = attention_fn(j, qa, k_f32, v_f32, k_b, v_b, k_means)
            w_mix, mix_layer = w["att_out"], j
        h = _outproj_ln(mix_in, w_mix, mix_layer, h, w["ln_g"][layer, 0], w["ln_b"][layer, 0], alpha, precise)
        f = _ffn_in(h, w["ffn_in"], layer, precise)
        h = _outproj_ln(f, w["ffn_out"], layer, h, w["ln_g"][layer, 1], w["ln_b"][layer, 1], alpha, precise)
    return h.reshape(n_seq, seq, dm), jnp.stack(states), k_f32, v_f32


def kernel(x_prompt, x_sample, cache_k, cache_v, state_ret, page_table, w_ret_in, w_ret_out, w_kv,
           w_q, w_att_out, w_ffn_in, w_ffn_out, ln_g, ln_b):
    depth = w_ffn_in.shape[0]
    n_b, seq, _ = x_prompt.shape
    n_d, t_n, _ = x_sample.shape
    ps = cache_k.shape[1]
    n_pages = page_table.shape[1]
    past_len = n_pages * ps
    ppb = MOBA_BLOCK // ps
    n_full = past_len // MOBA_BLOCK
    assert seq % MOBA_BLOCK == 0 and seq % RET_CHUNK == 0
    assert n_full * ppb == n_pages and n_full >= MOBA_TOPK
    assert t_n <= SAMPLE_CHUNK_PAD

    w32 = {"ret_in": w_ret_in, "ret_out": w_ret_out, "kv": w_kv, "q": w_q, "att_out": w_att_out,
           "ffn_in": w_ffn_in, "ffn_out": w_ffn_out, "ln_g": ln_g, "ln_b": ln_b}
    w = {name: (a if name.startswith("ln_") else a.astype(BF16)) for name, a in w32.items()}
    slopes = _alibi_slope_rows()
    cache_means = [None]

    def ret_prompt(layer, q, k, v, g):
        del layer
        c3 = lambda a: a.reshape(-1, RET_CHUNK, a.shape[-1])
        o, s = _retention(c3(q), c3(k), c3(v), c3(g), None, n_seq=n_b, c_real=RET_CHUNK)
        return o.reshape(n_b * seq, -1), s

    def att_prompt(j, q, k_f32, v_f32, k_b, v_b, means):
        del k_f32, v_f32
        if j == 0:
            o, cache_means[0] = _moba_prompt(q, k_b, v_b, means, slopes, n_seq=n_b,
                                             cache_k=cache_k, page_table=page_table)
        else:
            o, _ = _moba_prompt(q, k_b, v_b, means, slopes, n_seq=n_b)
        return o

    y_p, s_p, k_p, v_p = _decoder(x_prompt, ret_prompt, att_prompt, w, depth, True)

    def ret_sample(layer, q, k, v, g):
        def c3(a):
            a = a.reshape(n_d, t_n, a.shape[-1])
            return jnp.pad(a, ((0, 0), (0, SAMPLE_CHUNK_PAD - t_n), (0, 0)))
        o, s = _retention(c3(q), c3(k), c3(v), c3(g), state_ret[layer].astype(F32), n_seq=n_d, c_real=t_n,
                          precise=True)
        return o[:, :t_n].reshape(n_d * t_n, -1), s

    def att_sample(j, q, k_f32, v_f32, k_b, v_b, means):
        del j, k_b, v_b, means
        r3 = lambda a: a.reshape(n_d, t_n, a.shape[-1])
        if cache_means[0] is None:
            cache_means[0] = _cache_means(cache_k, page_table, n_full, ppb)
        idx, phys = _sample_topk(r3(q), cache_means[0].reshape(n_d, n_full, -1), page_table, ppb)
        idx_flat = idx[:, :, :ATT_HEADS * MOBA_TOPK].reshape(-1)
        phys_flat = phys[:, :, :ATT_HEADS * MOBA_TOPK * ppb].reshape(-1)
        o = _moba_sample(r3(q), r3(k_f32), r3(v_f32), cache_k, cache_v, idx_flat, phys_flat, slopes,
                         past_len=past_len)
        return o.reshape(n_d * t_n, -1)

    y_s, s_s, k_s, v_s = _decoder(x_sample, ret_sample, att_sample, w32, depth, False)

    dh = k_p.shape[-1] // ATT_HEADS
    kv4 = lambda a, n, l: a.reshape(n, l, ATT_HEADS, dh)
    return (y_p, y_s, s_p, s_s, kv4(k_p, n_b, seq), kv4(v_p, n_b, seq), kv4(k_s, n_d, t_n), kv4(v_s, n_d, t_n))
```
